```python
import math
import jax, jax.numpy as jnp
from jax import lax
import numpy as np

D_MODEL = 1024
BATCH = 8
SEQ = 2048
DEPTH = 2
DEC_BATCH = 32
DEC_SEQ = 4
PAST_LEN = 8192
PAGE_SIZE = 128

HEAD_DIM = 64
ATTN_HEADS = D_MODEL // 128
N_KV_HEADS = ATTN_HEADS // 4
ATTN_DIM = ATTN_HEADS * HEAD_DIM
KV_DIM = N_KV_HEADS * HEAD_DIM
IDX_HEADS = 4
IDX_DIM = 64
TOPK_MAX = 256
ROPE_THETA = 10000.0
Q_BLOCK = 64
SSD_HEAD_DIM = 64
SSD_HEADS = D_MODEL // 256
SSD_INNER = SSD_HEADS * SSD_HEAD_DIM
SSD_STATE = 128
CONV_W = 4
CONV_DIM = SSD_INNER + 2 * SSD_STATE
SSD_CHUNK = 128
POOL_WINDOWS = (2, 4, 8, 16)
POOL_GROUPS = 4
POOL_DIM = D_MODEL // 4
POOL_GROUP_DIM = POOL_DIM // POOL_GROUPS
POOL_BUF = 15
MIX_DIM = ATTN_DIM + SSD_INNER + POOL_DIM
IN_SIZES = (ATTN_DIM, KV_DIM, KV_DIM, IDX_HEADS * IDX_DIM, IDX_DIM, IDX_HEADS,
            SSD_INNER, CONV_DIM, SSD_HEADS, POOL_DIM)
IN_DIM = (ATTN_DIM + 2 * KV_DIM + IDX_HEADS * IDX_DIM + IDX_DIM + IDX_HEADS
          + SSD_INNER + CONV_DIM + SSD_HEADS + POOL_DIM)
D_FF = ((8 * D_MODEL + 3 * 256 - 1) // (3 * 256)) * 256
EPS = 1e-6

kernel_name = "hymba_dsa_ssd_pool_step"

F32 = jnp.float32


def _split_cols(h, sizes):
    offs = []
    acc = 0
    for s in sizes[:-1]:
        acc += s
        offs.append(acc)
    return jnp.split(h, offs, axis=-1)


def rmsnorm(x, g):
    xf = x.astype(F32)
    y = xf * lax.rsqrt(jnp.mean(xf * xf, axis=-1, keepdims=True) + EPS)
    return (y * g.astype(F32)).astype(x.dtype)


def rope(x, pos):
    half = x.shape[-1] // 2
    inv = ROPE_THETA ** (-jnp.arange(half, dtype=F32) / half)
    ang = pos.astype(F32)[:, None] * inv[None, :]
    cos = jnp.cos(ang)[None, :, None, :]
    sin = jnp.sin(ang)[None, :, None, :]
    xf = x.astype(F32)
    x1, x2 = xf[..., :half], xf[..., half:]
    return jnp.concatenate([x1 * cos - x2 * sin, x2 * cos + x1 * sin], axis=-1).astype(x.dtype)


def indexer_scores(q_idx, w_idx, k_idx):
    s = jnp.einsum('bthd,bsd->bths', q_idx.astype(F32), k_idx.astype(F32)) * (IDX_DIM ** -0.5)
    return jnp.einsum('bths,bth->bts', jax.nn.relu(s), w_idx.astype(F32) * (IDX_HEADS ** -0.5))


def sparse_attend(q, k_sel, v_sel, valid):
    Bsz, T, H, d = q.shape
    qg = q.reshape(Bsz, T, N_KV_HEADS, H // N_KV_HEADS, d).astype(F32)
    logits = jnp.einsum('btgrd,btkgd->btgrk', qg, k_sel.astype(F32)) * (d ** -0.5)
    logits = jnp.where(valid[:, :, None, None, :], logits, -jnp.inf)
    p = jax.nn.softmax(logits, axis=-1)
    o = jnp.einsum('btgrk,btkgd->btgrd', p, v_sel.astype(F32))
    return o.reshape(Bsz, T, H * d).astype(q.dtype)


_gather_rows = jax.vmap(lambda arr, ii: arr[ii])


def attn_prompt(q, k, v, q_idx, k_idx, w_idx):
    Bsz, S = q.shape[0], q.shape[1]
    topk = min(TOPK_MAX, S // 4)
    nb = S // Q_BLOCK
    key_pos = jnp.arange(S)

    def blk(args):
        qb, qib, wb, pos = args
        sc = indexer_scores(qib, wb, k_idx)
        sc = jnp.where((key_pos[None, :] <= pos[:, None])[None], sc, -jnp.inf)
        _, idx = lax.top_k(sc, topk)
        valid = idx <= pos[None, :, None]
        return sparse_attend(qb, _gather_rows(k, idx), _gather_rows(v, idx), valid)

    def to_blocks(a):
        return jnp.moveaxis(a.reshape((Bsz, nb, Q_BLOCK) + a.shape[2:]), 1, 0)

    pos_b = jnp.arange(S).reshape(nb, Q_BLOCK)
    out = lax.map(blk, (to_blocks(q), to_blocks(q_idx), to_blocks(w_idx), pos_b))
    return jnp.moveaxis(out, 0, 1).reshape(Bsz, S, ATTN_DIM)


def attn_sample(q, k, v, q_idx, k_idx, w_idx, cache_k, cache_v, cache_kidx, page_table):
    Bd, T = q.shape[0], q.shape[1]
    P = page_table.shape[1] * PAGE_SIZE
    L = P + T
    topk = min(TOPK_MAX, L // 4)
    kidx_past = cache_kidx[page_table].reshape(Bd, P, IDX_DIM)
    kidx_all = jnp.concatenate([kidx_past.astype(k_idx.dtype), k_idx], axis=1)
    sc = indexer_scores(q_idx, w_idx, kidx_all)
    qpos = P + jnp.arange(T)
    sc = jnp.where((jnp.arange(L)[None, :] <= qpos[:, None])[None], sc, -jnp.inf)
    _, idx = lax.top_k(sc, topk)
    valid = idx <= qpos[None, :, None]
    in_past = (idx < P)[..., None, None]
    pidx = jnp.minimum(idx, P - 1)
    phys = _gather_rows(page_table, pidx // PAGE_SIZE)
    off = pidx % PAGE_SIZE
    nidx = jnp.clip(idx - P, 0, T - 1)
    k_sel = jnp.where(in_past, cache_k[phys, off].astype(k.dtype), _gather_rows(k, nidx))
    v_sel = jnp.where(in_past, cache_v[phys, off].astype(v.dtype), _gather_rows(v, nidx))
    return sparse_attend(q, k_sel, v_sel, valid)


def segsum(a):
    Q = a.shape[-1]
    cs = jnp.cumsum(a, axis=-1)
    d = cs[..., :, None] - cs[..., None, :]
    return jnp.where(jnp.tril(jnp.ones((Q, Q), dtype=bool)), d, -jnp.inf)


def ssd_scan(x, dt, A, Bm, Cm, init_state):
    Bsz, L, H, Pd = x.shape
    N = Bm.shape[-1]
    Q = SSD_CHUNK if L % SSD_CHUNK == 0 else L
    nc = L // Q
    xdt = (x.astype(F32) * dt[..., None]).reshape(Bsz, nc, Q, H, Pd)
    a = jnp.transpose((dt * A).reshape(Bsz, nc, Q, H), (0, 3, 1, 2))
    Bc = Bm.astype(F32).reshape(Bsz, nc, Q, N)
    Cc = Cm.astype(F32).reshape(Bsz, nc, Q, N)
    a_cs = jnp.cumsum(a, axis=-1)
    Lm = jnp.exp(segsum(a))
    cb = jnp.einsum('bcln,bcsn->bcls', Cc, Bc)
    y_diag = jnp.einsum('bcls,bhcls,bcshp->bclhp', cb, Lm, xdt)
    decay_states = jnp.exp(a_cs[..., -1:] - a_cs)
    states = jnp.einsum('bcsn,bhcs,bcshp->bchpn', Bc, decay_states, xdt)
    states = jnp.concatenate([init_state.astype(F32)[:, None], states], axis=1)
    chunk_a = jnp.pad(a_cs[..., -1], ((0, 0), (0, 0), (1, 0)))
    decay_chunk = jnp.exp(segsum(chunk_a))
    new_states = jnp.einsum('bhzc,bchpn->bzhpn', decay_chunk, states)
    y_off = jnp.einsum('bcln,bchpn,bhcl->bclhp', Cc, new_states[:, :-1], jnp.exp(a_cs))
    return (y_diag + y_off).reshape(Bsz, L, H, Pd), new_states[:, -1]


def ssd_mixer(z, xbc, dt_raw, conv_buf, ssm_state, conv_w, conv_b, dt_bias, a_log, d_skip, norm_w):
    Bsz, L, _ = xbc.shape
    ext = jnp.concatenate([conv_buf.astype(xbc.dtype), xbc], axis=1)
    new_conv = ext[:, -(CONV_W - 1):]
    acc = conv_b.astype(F32)
    for j in range(CONV_W):
        acc = acc + ext[:, j:j + L].astype(F32) * conv_w[j].astype(F32)
    u = jax.nn.silu(acc)
    xs, Bm, Cm = jnp.split(u, [SSD_INNER, SSD_INNER + SSD_STATE], axis=-1)
    dt = jax.nn.softplus(dt_raw.astype(F32) + dt_bias.astype(F32))
    A = -jnp.exp(a_log.astype(F32))
    xh = xs.reshape(Bsz, L, SSD_HEADS, SSD_HEAD_DIM)
    y, final = ssd_scan(xh, dt, A, Bm, Cm, ssm_state)
    y = (y + d_skip.astype(F32)[:, None] * xh).reshape(Bsz, L, SSD_INNER)
    y = rmsnorm(y * jax.nn.silu(z.astype(F32)), norm_w)
    return y.astype(z.dtype), final.astype(z.dtype), new_conv


def pool_mixer(xp, buf, buf_valid, pool_w, pool_scale):
    Bsz, L, _ = xp.shape
    ext = jnp.concatenate([buf.astype(xp.dtype), xp], axis=1)
    cs = jnp.pad(jnp.cumsum(ext.astype(F32), axis=1), ((0, 0), (1, 0), (0, 0)))
    valid = jnp.concatenate([buf_valid.astype(F32), jnp.ones((L,), F32)])
    cnt = jnp.pad(jnp.cumsum(valid), (1, 0))
    end = POOL_BUF + 1
    means = []
    for g, w in enumerate(POOL_WINDOWS):
        sl = slice(g * POOL_GROUP_DIM, (g + 1) * POOL_GROUP_DIM)
        s_sum = cs[:, end:end + L, sl] - cs[:, end - w:end - w + L, sl]
        n = cnt[end:end + L] - cnt[end - w:end - w + L]
        means.append(s_sum / n[None, :, None])
    pooled = jnp.concatenate(means, axis=-1) - xp.astype(F32)
    mixed = jnp.einsum('blgc,gcd->blgd', pooled.reshape(Bsz, L, POOL_GROUPS, POOL_GROUP_DIM),
                       pool_w.astype(F32)).reshape(Bsz, L, POOL_DIM)
    return (mixed * pool_scale.astype(F32)).astype(xp.dtype), ext[:, -POOL_BUF:]


def layer_forward(x, c, pos, attend, conv_buf, ssm_state, pool_buf, pool_valid, w):
    Bsz, L, _ = x.shape
    mod = (jax.nn.silu(c) @ w['w_ada'] + w['b_ada'])[:, None, :]
    sh1, sc1, g1, sh2, sc2, g2 = jnp.split(mod, 6, axis=-1)
    h = rmsnorm(x, w['ln_mix']) * (1 + sc1) + sh1
    q, k, v, qi, ki, wi, z, xbc, dtr, xpool = _split_cols(h @ w['w_in'], IN_SIZES)
    q = rope(q.reshape(Bsz, L, ATTN_HEADS, HEAD_DIM), pos)
    k = rope(k.reshape(Bsz, L, N_KV_HEADS, HEAD_DIM), pos)
    v = v.reshape(Bsz, L, N_KV_HEADS, HEAD_DIM)
    qi = rope(qi.reshape(Bsz, L, IDX_HEADS, IDX_DIM), pos)
    ki = rope(ki[:, :, None, :], pos)[:, :, 0, :]
    a_out = attend(q, k, v, qi, ki, wi)
    s_out, new_ssm, new_conv = ssd_mixer(z, xbc, dtr, conv_buf, ssm_state, w['conv_w'], w['conv_b'],
                                         w['dt_bias'], w['a_log'], w['d_skip'], w['ssd_norm'])
    p_out, new_pool = pool_mixer(xpool, pool_buf, pool_valid, w['pool_w'], w['pool_scale'])
    mix = jnp.concatenate([a_out, s_out, p_out], axis=-1) @ w['w_out']
    x = x + g1 * mix
    h2 = rmsnorm(x, w['ln_ffn']) * (1 + sc2) + sh2
    ff = (jax.nn.silu(h2 @ w['w_gate']) * (h2 @ w['w_up'])) @ w['w_down']
    x = x + g2 * ff
    return x, k, v, ki, new_ssm, new_conv, new_pool


def setup_inputs(seed: int = 0) -> dict:
    key = jax.random.key(seed)
    ks = jax.random.split(key, 32)
    n_pages = PAST_LEN // PAGE_SIZE
    n_used = DEC_BATCH * n_pages
    n_pool = (n_used * 5) // 4

    def nrm(k, shape, s=1.0):
        return s * jax.random.normal(k, shape, F32)

    dt0 = jnp.exp(jax.random.uniform(ks[20], (DEPTH, SSD_HEADS), F32)
                  * (math.log(0.1) - math.log(0.001)) + math.log(0.001))
    return {
        'x_prompt': nrm(ks[0], (BATCH, SEQ, D_MODEL)),
        'x_sample': nrm(ks[1], (DEC_BATCH, DEC_SEQ, D_MODEL)),
        'cache_k': nrm(ks[2], (DEPTH, n_pool, PAGE_SIZE, N_KV_HEADS, HEAD_DIM)),
        'cache_v': nrm(ks[3], (DEPTH, n_pool, PAGE_SIZE, N_KV_HEADS, HEAD_DIM)),
        'cache_kidx': nrm(ks[4], (DEPTH, n_pool, PAGE_SIZE, IDX_DIM)),
        'state_ssm': nrm(ks[5], (DEPTH, DEC_BATCH, SSD_HEADS, SSD_HEAD_DIM, SSD_STATE), 0.5),
        'state_conv': nrm(ks[6], (DEPTH, DEC_BATCH, CONV_W - 1, CONV_DIM)),
        'state_pool': nrm(ks[7], (DEPTH, DEC_BATCH, POOL_BUF, POOL_DIM)),
        'page_table': jax.random.permutation(ks[8], n_pool)[:n_used].reshape(DEC_BATCH, n_pages).astype(jnp.int32),
        'c_prompt': nrm(ks[9], (BATCH, D_MODEL)),
        'c_sample': nrm(ks[10], (DEC_BATCH, D_MODEL)),
        'ln_mix': 1.0 + nrm(ks[11], (DEPTH, D_MODEL), 0.02),
        'ln_ffn': 1.0 + nrm(ks[12], (DEPTH, D_MODEL), 0.02),
        'w_ada': nrm(ks[13], (DEPTH, D_MODEL, 6 * D_MODEL), 0.5 * D_MODEL ** -0.5),
        'b_ada': nrm(ks[14], (DEPTH, 6 * D_MODEL), 0.02),
        'w_in': nrm(ks[15], (DEPTH, D_MODEL, IN_DIM), D_MODEL ** -0.5),
        'conv_w': nrm(ks[16], (DEPTH, CONV_W, CONV_DIM), CONV_W ** -0.5),
        'conv_b': nrm(ks[17], (DEPTH, CONV_DIM), 0.02),
        'dt_bias': dt0 + jnp.log(-jnp.expm1(-dt0)),
        'a_log': jnp.log(jax.random.uniform(ks[18], (DEPTH, SSD_HEADS), F32, minval=1.0, maxval=16.0)),
        'd_skip': 1.0 + nrm(ks[19], (DEPTH, SSD_HEADS), 0.1),
        'ssd_norm': 1.0 + nrm(ks[21], (DEPTH, SSD_INNER), 0.02),
        'pool_w': nrm(ks[22], (DEPTH, POOL_GROUPS, POOL_GROUP_DIM, POOL_GROUP_DIM), POOL_GROUP_DIM ** -0.5),
        'pool_scale': 1.0 + nrm(ks[23], (DEPTH, POOL_DIM), 0.1),
        'w_out': nrm(ks[24], (DEPTH, MIX_DIM, D_MODEL), MIX_DIM ** -0.5),
        'w_gate': nrm(ks[25], (DEPTH, D_MODEL, D_FF), D_MODEL ** -0.5),
        'w_up': nrm(ks[26], (DEPTH, D_MODEL, D_FF), D_MODEL ** -0.5),
        'w_down': nrm(ks[27], (DEPTH, D_FF, D_MODEL), D_FF ** -0.5),
        'ln_final': 1.0 + nrm(ks[28], (D_MODEL,), 0.02),
    }


def reference(x_prompt, x_sample, cache_k, cache_v, cache_kidx, state_ssm, state_conv, state_pool,
              page_table, c_prompt, c_sample, ln_mix, ln_ffn, w_ada, b_ada, w_in, conv_w, conv_b,
              dt_bias, a_log, d_skip, ssd_norm, pool_w, pool_scale, w_out, w_gate, w_up, w_down,
              ln_final):
    Bp, Lp = x_prompt.shape[0], x_prompt.shape[1]
    Bs, Ls = x_sample.shape[0], x_sample.shape[1]
    past = page_table.shape[1] * PAGE_SIZE
    pos_p = jnp.arange(Lp)
    pos_s = past + jnp.arange(Ls)
    dt = x_prompt.dtype
    zero_conv = jnp.zeros((Bp, CONV_W - 1, CONV_DIM), dt)
    zero_ssm = jnp.zeros((Bp, SSD_HEADS, SSD_HEAD_DIM, SSD_STATE), dt)
    zero_pool = jnp.zeros((Bp, POOL_BUF, POOL_DIM), dt)
    pool_valid_p = jnp.zeros((POOL_BUF,), F32)
    pool_valid_s = jnp.ones((POOL_BUF,), F32)

    xp, xs = x_prompt, x_sample
    pk, pv, pki, pssm, pconv, ppool = [], [], [], [], [], []
    sk, sv, ski, sssm, sconv, spool = [], [], [], [], [], []
    for l in range(DEPTH):
        w = {'ln_mix': ln_mix[l], 'ln_ffn': ln_ffn[l], 'w_ada': w_ada[l], 'b_ada': b_ada[l],
             'w_in': w_in[l], 'conv_w': conv_w[l], 'conv_b': conv_b[l], 'dt_bias': dt_bias[l],
             'a_log': a_log[l], 'd_skip': d_skip[l], 'ssd_norm': ssd_norm[l], 'pool_w': pool_w[l],
             'pool_scale': pool_scale[l], 'w_out': w_out[l], 'w_gate': w_gate[l], 'w_up': w_up[l],
             'w_down': w_down[l]}
        xp, k_, v_, ki_, ssm_, conv_, pool_ = layer_forward(
            xp, c_prompt, pos_p, attn_prompt, zero_conv, zero_ssm, zero_pool, pool_valid_p, w)
        pk.append(k_); pv.append(v_); pki.append(ki_); pssm.append(ssm_); pconv.append(conv_); ppool.append(pool_)

        ck, cv, cki = cache_k[l], cache_v[l], cache_kidx[l]

        def attend_s(q, k, v, qi, ki, wi, ck=ck, cv=cv, cki=cki):
            return attn_sample(q, k, v, qi, ki, wi, ck, cv, cki, page_table)

        xs, k_, v_, ki_, ssm_, conv_, pool_ = layer_forward(
            xs, c_sample, pos_s, attend_s, state_conv[l], state_ssm[l], state_pool[l], pool_valid_s, w)
        sk.append(k_); sv.append(v_); ski.append(ki_); sssm.append(ssm_); sconv.append(conv_); spool.append(pool_)

    y_prompt = rmsnorm(xp, ln_final)
    y_sample = rmsnorm(xs, ln_final)
    return (y_prompt, y_sample,
            jnp.stack(pk), jnp.stack(pv), jnp.stack(pki), jnp.stack(pssm), jnp.stack(pconv), jnp.stack(ppool),
            jnp.stack(sk), jnp.stack(sv), jnp.stack(ski), jnp.stack(sssm), jnp.stack(sconv), jnp.stack(spool))
```

```python
import functools
import math

import jax
import jax.numpy as jnp
from jax import lax
from jax.experimental import pallas as pl
from jax.experimental.pallas import tpu as pltpu

F32 = jnp.float32
BF16 = jnp.bfloat16
I32 = jnp.int32

D_MODEL = 1024
PAGE_SIZE = 128
HEAD_DIM = 64
ATTN_HEADS = 8
N_KV_HEADS = 2
ATTN_DIM = 512
KV_DIM = 128
IDX_HEADS = 4
IDX_DIM = 64
TOPK_MAX = 256
ROPE_THETA = 10000.0
SSD_HEAD_DIM = 64
SSD_HEADS = 4
SSD_INNER = 256
SSD_STATE = 128
CONV_W = 4
CONV_DIM = 512
POOL_WINDOWS = (2, 4, 8, 16)
POOL_DIM = 256
POOL_GROUP_DIM = 64
POOL_BUF = 15
D_FF = 2816
EPS = 1e-6

C_Q, C_K, C_V, C_QI, C_Z, C_XBC, C_XP, C_SM, C_END = 0, 512, 640, 768, 1024, 1280, 1792, 2048, 2176
SM_WI = 64
SM_DT = 68

LANES = 128
VMEM_LIMIT = 48 * 1024 * 1024
NEG_INF = float("-inf")
INT_MIN = -2 ** 31


def _dot(a, b, **kw):
    return jnp.dot(a, b, preferred_element_type=F32, **kw)


def _dot_nt(a, b):
    return lax.dot_general(a, b, (((1,), (1,)), ((), ())), preferred_element_type=F32)


def _silu(x):
    return x * jax.nn.sigmoid(x)


def _cparams(sem):
    return pltpu.CompilerParams(dimension_semantics=sem, vmem_limit_bytes=VMEM_LIMIT)


def _ada_kernel(c_ref, w_ref, b_ref, o_ref):
    a = _silu(c_ref[...])
    o_ref[...] = _dot(a, w_ref[...], precision=lax.Precision.HIGHEST) + b_ref[...]


def _ada(c_all, w_ada, b_ada):
    depth, d, n6 = w_ada.shape
    m = c_all.shape[0]
    tn = 1024
    return pl.pallas_call(
        _ada_kernel,
        grid=(depth, n6 // tn),
        in_specs=[pl.BlockSpec((m, d), lambda l, j: (0, 0)),
                  pl.BlockSpec((None, d, tn), lambda l, j: (l, 0, j)),
                  pl.BlockSpec((None, 1, tn), lambda l, j: (l, 0, j))],
        out_specs=pl.BlockSpec((None, m, tn), lambda l, j: (l, 0, j)),
        out_shape=jax.ShapeDtypeStruct((depth, m, n6), F32),
        compiler_params=_cparams(("arbitrary", "arbitrary")),
        name="ada",
    )(c_all, w_ada, b_ada.reshape(depth, 1, n6))


def _rms_mod(x, ln, sc, sh):
    y = x * lax.rsqrt(jnp.mean(x * x, axis=-1, keepdims=True) + EPS)
    return (y * ln) * (1.0 + sc) + sh


def _inproj_kernel(x_ref, ln_ref, sc_ref, sh_ref, w_ref, cos_ref, sin_ref,
                   q_ref, k_ref, v_ref, qi_ref, ki_ref, sm_ref, z_ref, xbc_ref, xp_ref,
                   kb_ref, vb_ref, kib_ref):
    h = _rms_mod(x_ref[...], ln_ref[...], sc_ref[...], sh_ref[...]).astype(BF16)
    cos = cos_ref[...]
    sin = sin_ref[...]
    lane = lax.broadcasted_iota(I32, (1, LANES), 1)
    first_half = (lane % HEAD_DIM) < (HEAD_DIM // 2)

    def proj(lo, hi):
        return _dot(h, w_ref[:, lo:hi])

    def rope(p):
        swapped = jnp.where(first_half, pltpu.roll(p, LANES - HEAD_DIM // 2, 1),
                            pltpu.roll(p, HEAD_DIM // 2, 1))
        return p * cos + swapped * sin

    scale = HEAD_DIM ** -0.5
    for c in range(ATTN_DIM // LANES):
        lo = C_Q + c * LANES
        q_ref[:, c * LANES:(c + 1) * LANES] = (rope(proj(lo, lo + LANES)) * scale).astype(BF16)
    k = rope(proj(C_K, C_V))
    k_ref[...] = k
    kb_ref[...] = k.astype(BF16)
    v = proj(C_V, C_QI)
    v_ref[...] = v
    vb_ref[...] = v.astype(BF16)
    for c in range(IDX_HEADS * IDX_DIM // LANES):
        lo = C_QI + c * LANES
        qi_ref[:, c * LANES:(c + 1) * LANES] = (rope(proj(lo, lo + LANES)) * scale).astype(BF16)
    z_ref[...] = proj(C_Z, C_XBC)
    xbc_ref[...] = proj(C_XBC, C_XP)
    xp_ref[...] = proj(C_XP, C_SM)
    sm = proj(C_SM, C_END)
    sm_ref[...] = sm
    ki = rope(sm)[:, :IDX_DIM]
    ki_ref[...] = ki
    kib_ref[...] = ki.astype(BF16)


def _inproj(x, ln, sc, sh, w, cos, sin, *, tl, rows_per_mod):
    n, d = x.shape
    nt = n // tl
    n_cos = cos.shape[0] // tl
    mod_rows = sc.shape[1]
    mod_spec = pl.BlockSpec((None, mod_rows, d), lambda i: (i // rows_per_mod, 0, 0))
    row = lambda width: pl.BlockSpec((tl, width), lambda i: (i, 0))
    outs = [(ATTN_DIM, BF16), (KV_DIM, F32), (KV_DIM, F32), (IDX_HEADS * IDX_DIM, BF16),
            (IDX_DIM, F32), (LANES, F32), (SSD_INNER, F32), (CONV_DIM, F32), (POOL_DIM, F32),
            (KV_DIM, BF16), (KV_DIM, BF16), (IDX_DIM, BF16)]
    return pl.pallas_call(
        _inproj_kernel,
        grid=(nt,),
        in_specs=[row(d),
                  pl.BlockSpec((1, d), lambda i: (0, 0)),
                  mod_spec, mod_spec,
                  pl.BlockSpec((d, C_END), lambda i: (0, 0)),
                  pl.BlockSpec((tl, LANES), lambda i: (i % n_cos, 0)),
                  pl.BlockSpec((tl, LANES), lambda i: (i % n_cos, 0))],
        out_specs=[row(wd) for wd, _ in outs],
        out_shape=[jax.ShapeDtypeStruct((n, wd), dt) for wd, dt in outs],
        compiler_params=_cparams(("arbitrary",)),
        name="inproj",
    )(x, ln, sc, sh, w, cos, sin)


def _sortable_key(score):
    score = jnp.where(score == 0.0, 0.0, score)
    bits = pltpu.bitcast(score, I32)
    return bits ^ ((bits >> 31) & 0x7FFFFFFF)


def _kth_largest_key(key_ref, kth):
    rows = key_ref.shape[0]

    def body(i, t_u):
        bit = jnp.left_shift(jnp.int32(1), 31 - i)
        cand_u = t_u | bit
        cand_s = cand_u ^ INT_MIN
        cnt = jnp.sum((key_ref[...] >= cand_s).astype(I32), axis=1, keepdims=True)
        return jnp.where(cnt >= kth, cand_u, t_u)

    t_u = lax.fori_loop(0, 32, body, jnp.zeros((rows, 1), I32))
    return t_u ^ INT_MIN


def _selection_bias(key_ref, bias_ref, tri_ref, allowed, kth, chunk):
    t = _kth_largest_key(key_ref, kth)
    n_gt = jnp.sum((key_ref[...] > t).astype(I32), axis=1, keepdims=True)
    quota = (kth - n_gt).astype(F32)
    run = jnp.zeros_like(quota)
    for c in range(key_ref.shape[1] // chunk):
        sl = slice(c * chunk, (c + 1) * chunk)
        kc = key_ref[:, sl]
        eq = jnp.where(kc == t, 1.0, 0.0)
        prefix = _dot(eq.astype(BF16), tri_ref[...]) + run
        run = prefix[:, chunk - 1:chunk]
        take = jnp.where(kc > t, 0.0, jnp.where(kc == t, jnp.where(prefix <= quota, 0.0, NEG_INF), NEG_INF))
        bias_ref[:, sl] = jnp.where(allowed[:, sl], take, NEG_INF)


def _upper_tri(n):
    r = jnp.arange(n)
    return (r[:, None] <= r[None, :]).astype(BF16)


def _attn_prompt_kernel(q_ref, qi_ref, sm_ref, k_ref, v_ref, ki_ref, tri_ref, o_ref,
                        key_s, bias_s, *, tq, seq, kth):
    j = pl.program_id(1)
    qi = qi_ref[...]
    ki = ki_ref[...]
    sm = sm_ref[...]
    score = jnp.zeros((tq, seq), F32)
    for h in range(IDX_HEADS):
        s = _dot_nt(qi[:, h * IDX_DIM:(h + 1) * IDX_DIM], ki)
        score = score + jnp.maximum(s, 0.0) * (sm[:, SM_WI + h:SM_WI + h + 1] * (IDX_HEADS ** -0.5))
    pos = j * tq + lax.broadcasted_iota(I32, (tq, 1), 0)
    col = lax.broadcasted_iota(I32, (tq, seq), 1)
    causal = col <= pos
    key_s[...] = _sortable_key(jnp.where(causal, score, NEG_INF))
    _selection_bias(key_s, bias_s, tri_ref, causal, kth, 256)

    q = q_ref[...]
    k = k_ref[...]
    v = v_ref[...]
    rep = ATTN_HEADS // N_KV_HEADS
    for h in range(ATTN_HEADS):
        g = h // rep
        logits = _dot_nt(q[:, h * HEAD_DIM:(h + 1) * HEAD_DIM], k[:, g * HEAD_DIM:(g + 1) * HEAD_DIM]) + bias_s[...]
        m = jnp.max(logits, axis=1, keepdims=True)
        e = jnp.exp(logits - m)
        den = jnp.sum(e, axis=1, keepdims=True)
        o = _dot(e.astype(BF16), v[:, g * HEAD_DIM:(g + 1) * HEAD_DIM]) / den
        o_ref[:, h * HEAD_DIM:(h + 1) * HEAD_DIM] = o.astype(BF16)


def _attn_prompt(q, qi, sm, kb, vb, kib, *, batch, seq, tq):
    n = q.shape[0]
    nq = seq // tq
    kth = min(TOPK_MAX, seq // 4)
    qrow = lambda width: pl.BlockSpec((tq, width), lambda b, j: (b * nq + j, 0))
    full = lambda width: pl.BlockSpec((seq, width), lambda b, j: (b, 0))
    return pl.pallas_call(
        functools.partial(_attn_prompt_kernel, tq=tq, seq=seq, kth=kth),
        grid=(batch, nq),
        in_specs=[qrow(ATTN_DIM), qrow(IDX_HEADS * IDX_DIM), qrow(LANES),
                  full(KV_DIM), full(KV_DIM), full(IDX_DIM),
                  pl.BlockSpec((256, 256), lambda b, j: (0, 0))],
        out_specs=qrow(ATTN_DIM),
        out_shape=jax.ShapeDtypeStruct((n, ATTN_DIM), BF16),
        scratch_shapes=[pltpu.VMEM((tq, seq), I32), pltpu.VMEM((tq, seq), F32)],
        compiler_params=_cparams(("arbitrary", "arbitrary")),
        name="attn_prompt",
    )(q, qi, sm, kb, vb, kib, _upper_tri(256))


T_PAD = 8
H_PAD = 8


def _attn_sample_kernel(pt_ref, qi_ref, wi_ref, q_ref, kn_ref, vn_ref, kin_ref, tri_ref,
                        ckidx_hbm, ck_hbm, cv_hbm, o_ref,
                        kidx_buf, k_buf, v_buf, key_s, bias_s, sems, *, n_pages, kth):
    b = pl.program_id(0)
    past = n_pages * PAGE_SIZE
    total = past + LANES

    def page_copies(p):
        page = pt_ref[b, p]
        return (pltpu.make_async_copy(ckidx_hbm.at[page], kidx_buf.at[p], sems.at[0]),
                pltpu.make_async_copy(ck_hbm.at[page], k_buf.at[p], sems.at[1]),
                pltpu.make_async_copy(cv_hbm.at[page], v_buf.at[p], sems.at[2]))

    def start(p, carry):
        for cp in page_copies(p):
            cp.start()
        return carry

    def wait(p, carry):
        for cp in page_copies(p):
            cp.wait()
        return carry

    lax.fori_loop(0, n_pages, start, 0)
    lax.fori_loop(0, n_pages, wait, 0)

    qi = qi_ref[...]
    kidx = kidx_buf[...].reshape(past, IDX_DIM).astype(BF16)
    s = jnp.concatenate([_dot_nt(qi, kidx), _dot_nt(qi, kin_ref[...])], axis=1)
    weighted = jnp.maximum(s, 0.0) * (wi_ref[...] * (IDX_HEADS ** -0.5))
    score = jnp.sum(weighted.reshape(T_PAD, H_PAD, total), axis=1)
    qpos = past + lax.broadcasted_iota(I32, (T_PAD, 1), 0)
    col = lax.broadcasted_iota(I32, (T_PAD, total), 1)
    causal = col <= qpos
    key_s[...] = _sortable_key(jnp.where(causal, score, NEG_INF))
    _selection_bias(key_s, bias_s, tri_ref, causal, kth, LANES)

    q = q_ref[...]
    kp = k_buf[...].reshape(past, KV_DIM).astype(BF16)
    vp = v_buf[...].reshape(past, KV_DIM).astype(BF16)
    bias = jnp.broadcast_to(bias_s[...][:, None, :], (T_PAD, ATTN_HEADS, total)).reshape(T_PAD * ATTN_HEADS, total)
    logits = jnp.concatenate([_dot_nt(q, kp), _dot_nt(q, kn_ref[...])], axis=1) + bias
    m = jnp.max(logits, axis=1, keepdims=True)
    e = jnp.exp(logits - m)
    den = jnp.sum(e, axis=1, keepdims=True)
    eb = e.astype(BF16)
    o = _dot(eb[:, :past], vp) + _dot(eb[:, past:], vn_ref[...])
    o_ref[...] = o / den


def _attn_sample(page_table, qi2, wi2, q2, kn, vn, kin, ckidx, ck, cv, *, steps):
    bd, n_pages = page_table.shape
    past = n_pages * PAGE_SIZE
    total = past + LANES
    kth = min(TOPK_MAX, (past + steps) // 4)
    per_b = lambda r, c: pl.BlockSpec((None, r, c), lambda b, pt: (b, 0, 0))
    grid_spec = pltpu.PrefetchScalarGridSpec(
        num_scalar_prefetch=1,
        grid=(bd,),
        in_specs=[per_b(T_PAD * H_PAD, IDX_DIM), per_b(T_PAD * H_PAD, 1), per_b(T_PAD * ATTN_HEADS, KV_DIM),
                  per_b(LANES, KV_DIM), per_b(LANES, KV_DIM), per_b(LANES, IDX_DIM),
                  pl.BlockSpec((LANES, LANES), lambda b, pt: (0, 0)),
                  pl.BlockSpec(memory_space=pl.ANY), pl.BlockSpec(memory_space=pl.ANY),
                  pl.BlockSpec(memory_space=pl.ANY)],
        out_specs=per_b(T_PAD * ATTN_HEADS, KV_DIM),
        scratch_shapes=[pltpu.VMEM((n_pages, PAGE_SIZE, IDX_DIM), F32),
                        pltpu.VMEM((n_pages, PAGE_SIZE, KV_DIM), F32),
                        pltpu.VMEM((n_pages, PAGE_SIZE, KV_DIM), F32),
                        pltpu.VMEM((T_PAD, total), I32),
                        pltpu.VMEM((T_PAD, total), F32),
                        pltpu.SemaphoreType.DMA((3,))])
    return pl.pallas_call(
        functools.partial(_attn_sample_kernel, n_pages=n_pages, kth=kth),
        grid_spec=grid_spec,
        out_shape=jax.ShapeDtypeStruct((bd, T_PAD * ATTN_HEADS, KV_DIM), F32),
        compiler_params=_cparams(("arbitrary",)),
        name="attn_sample",
    )(page_table, qi2, wi2, q2, kn, vn, kin, _upper_tri(LANES), ckidx, ck, cv)


def _expand_heads(v, first_lane):
    q = v.shape[0]
    lane = lax.broadcasted_iota(I32, (1, SSD_INNER), 1)
    out = jnp.broadcast_to(v[:, first_lane + SSD_HEADS - 1:first_lane + SSD_HEADS], (q, SSD_INNER))
    for h in range(SSD_HEADS - 2, -1, -1):
        out = jnp.where(lane < (h + 1) * SSD_HEAD_DIM, v[:, first_lane + h:first_lane + h + 1], out)
    return out


def _ssd_kernel(xbc_ref, z_ref, sm_ref, cbuf_ref, st0_ref, cw_ref, cb_ref, dtb_ref, alog_ref,
                dsk_ref, nw_ref, y_ref, stout_ref, xe_s, st_s, *, q, valid_len):
    c = pl.program_id(1)

    @pl.when(c == 0)
    def _():
        xe_s[0:8] = cbuf_ref[...]
        st_s[...] = st0_ref[...]

    xe_s[8:8 + q] = xbc_ref[...]
    acc = cb_ref[...]
    for j in range(CONV_W):
        acc = acc + xe_s[pl.ds(8 - (CONV_W - 1) + j, q), :] * cw_ref[j:j + 1, :]
    xe_s[0:8] = xe_s[q:q + 8]
    u = _silu(acc)
    xs = u[:, :SSD_INNER]
    bm = u[:, SSD_INNER:SSD_INNER + SSD_STATE].astype(BF16)
    cm = u[:, SSD_INNER + SSD_STATE:].astype(BF16)

    lane = lax.broadcasted_iota(I32, (1, LANES), 1)
    head_lanes = (lane >= SM_DT) & (lane < SM_DT + SSD_HEADS)
    pre = sm_ref[...] + dtb_ref[...]
    dt = jnp.maximum(pre, 0.0) + jnp.log1p(jnp.exp(-jnp.abs(pre)))
    dt = jnp.where(head_lanes, dt, 0.0)
    if valid_len < q:
        row_id = lax.broadcasted_iota(I32, (q, 1), 0)
        dt = jnp.where(row_id < valid_len, dt, 0.0)
    a = dt * jnp.where(head_lanes, -jnp.exp(alog_ref[...]), 0.0)

    r = lax.broadcasted_iota(I32, (q, q), 0)
    cidx = lax.broadcasted_iota(I32, (q, q), 1)
    lower = r >= cidx
    a_cs = _dot(jnp.where(lower, 1.0, 0.0), a, precision=lax.Precision.HIGHEST)
    a_cs_t = a_cs.T
    a_last = a_cs[q - 1:q, :]

    xdt = xs * _expand_heads(dt, SM_DT)
    cb = _dot_nt(cm, bm)
    lane_in = lax.broadcasted_iota(I32, (1, SSD_INNER), 1)
    y = jnp.zeros((q, SSD_INNER), F32)
    for h in range(SSD_HEADS):
        seg = a_cs[:, SM_DT + h:SM_DT + h + 1] - a_cs_t[SM_DT + h:SM_DT + h + 1, :]
        lmat = jnp.exp(jnp.where(lower, seg, NEG_INF))
        in_head = (lane_in >= h * SSD_HEAD_DIM) & (lane_in < (h + 1) * SSD_HEAD_DIM)
        y = y + _dot((cb * lmat).astype(BF16), jnp.where(in_head, xdt, 0.0).astype(BF16))

    st = st_s[...]
    y = y + _dot_nt(cm, st.astype(BF16)) * _expand_heads(jnp.exp(a_cs), SM_DT)
    decayed = xdt * _expand_heads(jnp.exp(a_last - a_cs), SM_DT)
    contrib = _dot(decayed.T.astype(BF16), bm)
    carry = jnp.exp(a_last)
    row_h = lax.broadcasted_iota(I32, (SSD_INNER, 1), 0) // SSD_HEAD_DIM
    fac = jnp.broadcast_to(carry[:, SM_DT + SSD_HEADS - 1:SM_DT + SSD_HEADS], (SSD_INNER, SSD_STATE))
    for h in range(SSD_HEADS - 2, -1, -1):
        fac = jnp.where(row_h == h, jnp.broadcast_to(carry[:, SM_DT + h:SM_DT + h + 1], (SSD_INNER, SSD_STATE)), fac)
    st_new = st * fac + contrib
    st_s[...] = st_new

    y = y + dsk_ref[...] * xs
    gated = y * _silu(z_ref[...])
    out = gated * lax.rsqrt(jnp.mean(gated * gated, axis=-1, keepdims=True) + EPS) * nw_ref[...]
    y_ref[...] = out.astype(BF16)

    @pl.when(c == pl.num_programs(1) - 1)
    def _():
        stout_ref[...] = st_new


def _ssd(xbc, z, sm, cbuf, st0, cw, cb, dtb, alog, dsk, nw, *, batch, seq, q, valid_len):
    n = xbc.shape[0]
    nc = seq // q
    row = lambda width: pl.BlockSpec((q, width), lambda b, c: (b * nc + c, 0))
    const = lambda r, w: pl.BlockSpec((r, w), lambda b, c: (0, 0))
    return pl.pallas_call(
        functools.partial(_ssd_kernel, q=q, valid_len=valid_len),
        grid=(batch, nc),
        in_specs=[row(CONV_DIM), row(SSD_INNER), row(LANES),
                  pl.BlockSpec((None, 8, CONV_DIM), lambda b, c: (b, 0, 0)),
                  pl.BlockSpec((None, SSD_INNER, SSD_STATE), lambda b, c: (b, 0, 0)),
                  const(CONV_W, CONV_DIM), const(1, CONV_DIM), const(1, LANES), const(1, LANES),
                  const(1, SSD_INNER), const(1, SSD_INNER)],
        out_specs=[row(SSD_INNER),
                   pl.BlockSpec((None, SSD_INNER, SSD_STATE), lambda b, c: (b, 0, 0))],
        out_shape=[jax.ShapeDtypeStruct((n, SSD_INNER), BF16),
                   jax.ShapeDtypeStruct((batch, SSD_INNER, SSD_STATE), F32)],
        scratch_shapes=[pltpu.VMEM((q + 8, CONV_DIM), F32), pltpu.VMEM((SSD_INNER, SSD_STATE), F32)],
        compiler_params=_cparams(("arbitrary", "arbitrary")),
        name="ssd",
    )(xbc, z, sm, cbuf, st0, cw, cb, dtb, alog, dsk, nw)


def _window_sums(e):
    s2 = e[0] + e[1]
    s4 = s2 + e[2] + e[3]
    s8 = s4 + e[4] + e[5] + e[6] + e[7]
    s16 = s8
    for k in range(8, 16):
        s16 = s16 + e[k]
    lane = lax.broadcasted_iota(I32, (1, POOL_DIM), 1)
    g = POOL_GROUP_DIM
    ssum = jnp.where(lane < g, s2, jnp.where(lane < 2 * g, s4, jnp.where(lane < 3 * g, s8, s16)))
    width = jnp.where(lane < g, 2.0, jnp.where(lane < 2 * g, 4.0, jnp.where(lane < 3 * g, 8.0, 16.0)))
    return ssum, width


def _pool_prompt_kernel(xp_ref, w_ref, scale_ref, o_ref, ext_s, *, tl):
    i = pl.program_id(1)
    halo = POOL_BUF + 1

    @pl.when(i == 0)
    def _():
        ext_s[0:halo] = jnp.zeros((halo, POOL_DIM), F32)

    ext_s[halo:halo + tl] = xp_ref[...]
    e = [ext_s[pl.ds(halo - k, tl), :] for k in range(halo)]
    ssum, width = _window_sums(e)
    t = (i * tl + lax.broadcasted_iota(I32, (tl, 1), 0)).astype(F32)
    count = jnp.minimum(t + 1.0, width)
    pooled = ssum / count - e[0]
    o_ref[...] = (_dot(pooled.astype(BF16), w_ref[...]) * scale_ref[...]).astype(BF16)
    ext_s[0:halo] = ext_s[tl:tl + halo]


def _pool_prompt(xp, w_bd, scale, *, batch, seq, tl):
    n = xp.shape[0]
    nt = seq // tl
    return pl.pallas_call(
        functools.partial(_pool_prompt_kernel, tl=tl),
        grid=(batch, nt),
        in_specs=[pl.BlockSpec((tl, POOL_DIM), lambda b, i: (b * nt + i, 0)),
                  pl.BlockSpec((POOL_DIM, POOL_DIM), lambda b, i: (0, 0)),
                  pl.BlockSpec((1, POOL_DIM), lambda b, i: (0, 0))],
        out_specs=pl.BlockSpec((tl, POOL_DIM), lambda b, i: (b * nt + i, 0)),
        out_shape=jax.ShapeDtypeStruct((n, POOL_DIM), BF16),
        scratch_shapes=[pltpu.VMEM((tl + POOL_BUF + 1, POOL_DIM), F32)],
        compiler_params=_cparams(("arbitrary", "arbitrary")),
        name="pool_prompt",
    )(xp, w_bd, scale)


def _pool_sample_kernel(ext_ref, w_ref, scale_ref, o_ref, *, steps):
    e = [ext_ref[POOL_BUF - k:POOL_BUF - k + steps] for k in range(POOL_BUF + 1)]
    ssum, width = _window_sums(e)
    pooled = ssum / width - e[0]
    rows = pooled.shape[0] * pooled.shape[1]
    mixed = _dot(pooled.reshape(rows, POOL_DIM).astype(BF16), w_ref[...])
    o_ref[...] = (mixed * scale_ref[...]).astype(BF16)


def _pool_sample(ext_t, w_bd, scale, *, steps):
    rows = steps * ext_t.shape[1]
    return pl.pallas_call(
        functools.partial(_pool_sample_kernel, steps=steps),
        out_shape=jax.ShapeDtypeStruct((rows, POOL_DIM), BF16),
        compiler_params=pltpu.CompilerParams(vmem_limit_bytes=VMEM_LIMIT),
        name="pool_sample",
    )(ext_t, w_bd, scale)


def _mix_ffn_kernel(x_ref, a_ref, s_ref, p_ref, wo_ref, g1_ref, ln_ref, sc_ref, sh_ref, g2_ref,
                    wg_ref, wu_ref, wd_ref, lnf_ref, o_ref, x1_s, h2_s, acc_s, *, final_norm):
    f = pl.program_id(1)

    @pl.when(f == 0)
    def _():
        mix = (_dot(a_ref[...], wo_ref[0:ATTN_DIM, :])
               + _dot(s_ref[...], wo_ref[ATTN_DIM:ATTN_DIM + SSD_INNER, :])
               + _dot(p_ref[...], wo_ref[ATTN_DIM + SSD_INNER:, :]))
        x1 = x_ref[...] + g1_ref[...] * mix
        x1_s[...] = x1
        h2_s[...] = _rms_mod(x1, ln_ref[...], sc_ref[...], sh_ref[...]).astype(BF16)
        acc_s[...] = jnp.zeros_like(acc_s)

    h2 = h2_s[...]
    act = (_silu(_dot(h2, wg_ref[...])) * _dot(h2, wu_ref[...])).astype(BF16)
    acc_s[...] += _dot(act, wd_ref[...])

    @pl.when(f == pl.num_programs(1) - 1)
    def _():
        out = x1_s[...] + g2_ref[...] * acc_s[...]
        if final_norm:
            out = out * lax.rsqrt(jnp.mean(out * out, axis=-1, keepdims=True) + EPS) * lnf_ref[...]
        o_ref[...] = out


def _mix_ffn(x, a, s, p, wo, g1, ln, sc, sh, g2, wg, wu, wd, lnf, *, tl, rows_per_mod, nf, final_norm):
    n, d = x.shape
    nt = n // tl
    tf = D_FF // nf
    mod_rows = g1.shape[1]
    mod_spec = pl.BlockSpec((None, mod_rows, d), lambda i, f: (i // rows_per_mod, 0, 0))
    row = lambda width: pl.BlockSpec((tl, width), lambda i, f: (i, 0))
    vec = pl.BlockSpec((1, d), lambda i, f: (0, 0))
    return pl.pallas_call(
        functools.partial(_mix_ffn_kernel, final_norm=final_norm),
        grid=(nt, nf),
        in_specs=[row(d), row(ATTN_DIM), row(SSD_INNER), row(POOL_DIM),
                  pl.BlockSpec((d, d), lambda i, f: (0, 0)),
                  mod_spec, vec, mod_spec, mod_spec, mod_spec,
                  pl.BlockSpec((d, tf), lambda i, f: (0, f)),
                  pl.BlockSpec((d, tf), lambda i, f: (0, f)),
                  pl.BlockSpec((tf, d), lambda i, f: (f, 0)),
                  vec],
        out_specs=row(d),
        out_shape=jax.ShapeDtypeStruct((n, d), F32),
        scratch_shapes=[pltpu.VMEM((tl, d), F32), pltpu.VMEM((tl, d), BF16), pltpu.VMEM((tl, d), F32)],
        compiler_params=_cparams(("arbitrary", "arbitrary")),
        name="mix_ffn",
    )(x, a, s, p, wo, g1, ln, sc, sh, g2, wg, wu, wd, lnf)


def _rope_tables(pos):
    half = HEAD_DIM // 2
    inv = ROPE_THETA ** (-jnp.arange(half, dtype=F32) / half)
    ang = pos.astype(F32)[:, None] * inv[None, :]
    cos, sin = jnp.cos(ang), jnp.sin(ang)
    cos_t = jnp.concatenate([cos, cos, cos, cos], axis=1)
    sin_t = jnp.concatenate([-sin, sin, -sin, sin], axis=1)
    return cos_t, sin_t


def _permute_w_in(w_in):
    sizes = (ATTN_DIM, KV_DIM, KV_DIM, IDX_HEADS * IDX_DIM, IDX_DIM, IDX_HEADS,
             SSD_INNER, CONV_DIM, SSD_HEADS, POOL_DIM)
    offs = [0]
    for s in sizes:
        offs.append(offs[-1] + s)
    seg = lambda i: w_in[:, offs[i]:offs[i + 1]]
    q, k, v, qi, ki, wi, z, xbc, dtr, xp = [seg(i) for i in range(len(sizes))]
    pad = jnp.zeros((w_in.shape[0], C_END - C_SM - IDX_DIM - IDX_HEADS - SSD_HEADS), w_in.dtype)
    return jnp.concatenate([q, k, v, qi, z, xbc, xp, ki, wi, dtr, pad], axis=1).astype(BF16)


def _block_diag(pool_w):
    g, c, _ = pool_w.shape
    out = jnp.zeros((g * c, g * c), pool_w.dtype)
    for i in range(g):
        out = out.at[i * c:(i + 1) * c, i * c:(i + 1) * c].set(pool_w[i])
    return out.astype(BF16)


def _head_lane_vec(v):
    return jnp.zeros((1, LANES), F32).at[0, SM_DT:SM_DT + SSD_HEADS].set(v)


def kernel(x_prompt, x_sample, cache_k, cache_v, cache_kidx, state_ssm, state_conv, state_pool,
           page_table, c_prompt, c_sample, ln_mix, ln_ffn, w_ada, b_ada, w_in, conv_w, conv_b,
           dt_bias, a_log, d_skip, ssd_norm, pool_w, pool_scale, w_out, w_gate, w_up, w_down,
           ln_final):
    bp, lp, d = x_prompt.shape
    bs, ls, _ = x_sample.shape
    depth = w_in.shape[0]
    n_pages = page_table.shape[1]
    past = n_pages * PAGE_SIZE
    n_p, n_s = bp * lp, bs * ls
    tl_p = 512
    ssd_q = 128

    cos_p, sin_p = _rope_tables(jnp.arange(lp))
    cos_s, sin_s = _rope_tables(past + jnp.arange(ls))
    cos_s, sin_s = jnp.tile(cos_s, (bs, 1)), jnp.tile(sin_s, (bs, 1))

    mod = _ada(jnp.concatenate([c_prompt, c_sample], axis=0), w_ada, b_ada)
    lnf = ln_final.reshape(1, d)

    xp = x_prompt.reshape(n_p, d)
    xs = x_sample.reshape(n_s, d)
    outs_p = [[] for _ in range(6)]
    outs_s = [[] for _ in range(6)]
    for l in range(depth):
        last = l == depth - 1
        mods_p = [m.reshape(bp, 1, d) for m in jnp.split(mod[l, :bp], 6, axis=-1)]
        mods_s = [jnp.repeat(m, ls, axis=0).reshape(1, n_s, d) for m in jnp.split(mod[l, bp:], 6, axis=-1)]
        w_in_p = _permute_w_in(w_in[l])
        wo = w_out[l].astype(BF16)
        wg, wu, wdn = w_gate[l].astype(BF16), w_up[l].astype(BF16), w_down[l].astype(BF16)
        w_bd = _block_diag(pool_w[l])
        p_scale = pool_scale[l].reshape(1, POOL_DIM)
        ln1, ln2 = ln_mix[l].reshape(1, d), ln_ffn[l].reshape(1, d)
        ssd_w = (conv_w[l], conv_b[l].reshape(1, CONV_DIM), _head_lane_vec(dt_bias[l]), _head_lane_vec(a_log[l]),
                 jnp.repeat(d_skip[l], SSD_HEAD_DIM).reshape(1, SSD_INNER), ssd_norm[l].reshape(1, SSD_INNER))

        sh1, sc1, g1, sh2, sc2, g2 = mods_p
        q, k, v, qi, ki, sm, z, xbc, xpool, kb, vb, kib = _inproj(
            xp, ln1, sc1, sh1, w_in_p, cos_p, sin_p, tl=tl_p, rows_per_mod=lp // tl_p)
        a_out = _attn_prompt(q, qi, sm, kb, vb, kib, batch=bp, seq=lp, tq=128)
        s_out, ssm = _ssd(xbc, z, sm, jnp.zeros((bp, 8, CONV_DIM), F32),
                          jnp.zeros((bp, SSD_INNER, SSD_STATE), F32), *ssd_w,
                          batch=bp, seq=lp, q=ssd_q, valid_len=ssd_q)
        p_out = _pool_prompt(xpool, w_bd, p_scale, batch=bp, seq=lp, tl=tl_p)
        xp = _mix_ffn(xp, a_out, s_out, p_out, wo, g1, ln2, sc2, sh2, g2, wg, wu, wdn, lnf,
                      tl=tl_p, rows_per_mod=lp // tl_p, nf=2, final_norm=last)
        outs_p[0].append(k.reshape(bp, lp, N_KV_HEADS, HEAD_DIM))
        outs_p[1].append(v.reshape(bp, lp, N_KV_HEADS, HEAD_DIM))
        outs_p[2].append(ki.reshape(bp, lp, IDX_DIM))
        outs_p[3].append(ssm.reshape(bp, SSD_HEADS, SSD_HEAD_DIM, SSD_STATE))
        outs_p[4].append(xbc.reshape(bp, lp, CONV_DIM)[:, lp - (CONV_W - 1):])
        outs_p[5].append(xpool.reshape(bp, lp, POOL_DIM)[:, lp - POOL_BUF:])

        sh1, sc1, g1, sh2, sc2, g2 = mods_s
        q, k, v, qi, ki, sm, z, xbc, xpool, kb, vb, kib = _inproj(
            xs, ln1, sc1, sh1, w_in_p, cos_s, sin_s, tl=n_s, rows_per_mod=1)
        tpad = ((0, 0), (0, T_PAD - ls))
        qi2 = jnp.pad(qi.reshape(bs, ls, IDX_HEADS, IDX_DIM), tpad + ((0, H_PAD - IDX_HEADS), (0, 0)))
        qi2 = qi2.reshape(bs, T_PAD * H_PAD, IDX_DIM)
        wi2 = jnp.pad(sm[:, SM_WI:SM_WI + IDX_HEADS].reshape(bs, ls, IDX_HEADS), tpad + ((0, H_PAD - IDX_HEADS),))
        wi2 = wi2.reshape(bs, T_PAD * H_PAD, 1)
        qh = jnp.pad(q.reshape(bs, ls, N_KV_HEADS, ATTN_HEADS // N_KV_HEADS, HEAD_DIM), tpad + ((0, 0),) * 3)
        eye = jnp.eye(N_KV_HEADS, dtype=q.dtype)
        q2 = jnp.einsum('btgrd,gj->btgrjd', qh, eye).reshape(bs, T_PAD * ATTN_HEADS, KV_DIM)
        rpad = ((0, 0), (0, LANES - ls), (0, 0))
        kn = jnp.pad(kb.reshape(bs, ls, KV_DIM), rpad)
        vn = jnp.pad(vb.reshape(bs, ls, KV_DIM), rpad)
        kin = jnp.pad(kib.reshape(bs, ls, IDX_DIM), rpad)
        n_pool = cache_k.shape[1]
        o2 = _attn_sample(page_table, qi2, wi2, q2, kn, vn, kin, cache_kidx[l],
                          cache_k[l].reshape(n_pool, PAGE_SIZE, KV_DIM),
                          cache_v[l].reshape(n_pool, PAGE_SIZE, KV_DIM), steps=ls)
        o2 = o2.reshape(bs, T_PAD, N_KV_HEADS, ATTN_HEADS // N_KV_HEADS, N_KV_HEADS, HEAD_DIM)[:, :ls]
        a_out = jnp.einsum('btgrjd,gj->btgrd', o2, jnp.eye(N_KV_HEADS, dtype=o2.dtype))
        a_out = a_out.reshape(n_s, ATTN_DIM).astype(BF16)
        seq_pad = lambda t: jnp.pad(t.reshape(bs, ls, -1), ((0, 0), (0, ssd_q - ls), (0, 0))).reshape(bs * ssd_q, -1)
        s_pad, ssm = _ssd(seq_pad(xbc), seq_pad(z), seq_pad(sm),
                          jnp.pad(state_conv[l], ((0, 0), (8 - (CONV_W - 1), 0), (0, 0))),
                          state_ssm[l].reshape(bs, SSD_INNER, SSD_STATE), *ssd_w,
                          batch=bs, seq=ssd_q, q=ssd_q, valid_len=ls)
        s_out = s_pad.reshape(bs, ssd_q, SSD_INNER)[:, :ls].reshape(n_s, SSD_INNER)
        ext = jnp.concatenate([state_pool[l], xpool.reshape(bs, ls, POOL_DIM)], axis=1)
        p_t = _pool_sample(jnp.transpose(ext, (1, 0, 2)), w_bd, p_scale, steps=ls)
        p_out = jnp.transpose(p_t.reshape(ls, bs, POOL_DIM), (1, 0, 2)).reshape(n_s, POOL_DIM)
        xs = _mix_ffn(xs, a_out, s_out, p_out, wo, g1, ln2, sc2, sh2, g2, wg, wu, wdn, lnf,
                      tl=n_s, rows_per_mod=1, nf=2, final_norm=last)
        outs_s[0].append(k.reshape(bs, ls, N_KV_HEADS, HEAD_DIM))
        outs_s[1].append(v.reshape(bs, ls, N_KV_HEADS, HEAD_DIM))
        outs_s[2].append(ki.reshape(bs, ls, IDX_DIM))
        outs_s[3].append(ssm.reshape(bs, SSD_HEADS, SSD_HEAD_DIM, SSD_STATE))
        outs_s[4].append(xbc.reshape(bs, ls, CONV_DIM)[:, ls - (CONV_W - 1):])
        outs_s[5].append(ext[:, ext.shape[1] - POOL_BUF:])

    return ((xp.reshape(bp, lp, d), xs.reshape(bs, ls, d))
            + tuple(jnp.stack(o) for o in outs_p) + tuple(jnp.stack(o) for o in outs_s))
```

```python
import functools
import math

import jax
import jax.numpy as jnp
from jax import lax
from jax.experimental import pallas as pl
from jax.experimental.pallas import tpu as pltpu

F32 = jnp.float32
BF16 = jnp.bfloat16
I32 = jnp.int32

D_MODEL = 1024
PAGE_SIZE = 128
HEAD_DIM = 64
ATTN_HEADS = 8
N_KV_HEADS = 2
ATTN_DIM = 512
KV_DIM = 128
IDX_HEADS = 4
IDX_DIM = 64
TOPK_MAX = 256
ROPE_THETA = 10000.0
SSD_HEAD_DIM = 64
SSD_HEADS = 4
SSD_INNER = 256
SSD_STATE = 128
CONV_W = 4
CONV_DIM = 512
POOL_WINDOWS = (2, 4, 8, 16)
POOL_DIM = 256
POOL_GROUP_DIM = 64
POOL_BUF = 15
D_FF = 2816
EPS = 1e-6

C_Q, C_K, C_V, C_QI, C_Z, C_XBC, C_XP, C_SM, C_END = 0, 512, 640, 768, 1024, 1280, 1792, 2048, 2176
SM_WI = 64
SM_DT = 68

LANES = 128
VMEM_LIMIT = 48 * 1024 * 1024
NEG_INF = float("-inf")
INT_MIN = -2 ** 31


def _dot(a, b, **kw):
    return jnp.dot(a, b, preferred_element_type=F32, **kw)


def _dot_nt(a, b):
    return lax.dot_general(a, b, (((1,), (1,)), ((), ())), preferred_element_type=F32)


def _silu(x):
    return x * jax.nn.sigmoid(x)


def _cparams(sem):
    return pltpu.CompilerParams(dimension_semantics=sem, vmem_limit_bytes=VMEM_LIMIT)


def _ada_kernel(c_ref, w_ref, b_ref, o_ref):
    a = _silu(c_ref[...])
    o_ref[...] = _dot(a, w_ref[...], precision=lax.Precision.HIGHEST) + b_ref[...]


def _ada(c_all, w_ada, b_ada):
    depth, d, n6 = w_ada.shape
    m = c_all.shape[0]
    tn = 1024
    return pl.pallas_call(
        _ada_kernel,
        grid=(depth, n6 // tn),
        in_specs=[pl.BlockSpec((m, d), lambda l, j: (0, 0)),
                  pl.BlockSpec((None, d, tn), lambda l, j: (l, 0, j)),
                  pl.BlockSpec((None, 1, tn), lambda l, j: (l, 0, j))],
        out_specs=pl.BlockSpec((None, m, tn), lambda l, j: (l, 0, j)),
        out_shape=jax.ShapeDtypeStruct((depth, m, n6), F32),
        compiler_params=_cparams(("arbitrary", "arbitrary")),
        name="ada",
    )(c_all, w_ada, b_ada.reshape(depth, 1, n6))


def _rms_mod(x, ln, sc, sh):
    y = x * lax.rsqrt(jnp.mean(x * x, axis=-1, keepdims=True) + EPS)
    return (y * ln) * (1.0 + sc) + sh


def _inproj_kernel(x_ref, ln_ref, sc_ref, sh_ref, w_ref, cos_ref, sin_ref,
                   q_ref, k_ref, v_ref, qi_ref, ki_ref, sm_ref, z_ref, xbc_ref, xp_ref,
                   kb_ref, vb_ref, kib_ref):
    h = _rms_mod(x_ref[...], ln_ref[...], sc_ref[...], sh_ref[...]).astype(BF16)
    cos = cos_ref[...]
    sin = sin_ref[...]
    lane = lax.broadcasted_iota(I32, (1, LANES), 1)
    first_half = (lane % HEAD_DIM) < (HEAD_DIM // 2)

    def proj(lo, hi):
        return _dot(h, w_ref[:, lo:hi])

    def rope(p):
        swapped = jnp.where(first_half, pltpu.roll(p, LANES - HEAD_DIM // 2, 1),
                            pltpu.roll(p, HEAD_DIM // 2, 1))
        return p * cos + swapped * sin

    scale = HEAD_DIM ** -0.5
    for c in range(ATTN_DIM // LANES):
        lo = C_Q + c * LANES
        q_ref[:, c * LANES:(c + 1) * LANES] = (rope(proj(lo, lo + LANES)) * scale).astype(BF16)
    k = rope(proj(C_K, C_V))
    k_ref[...] = k
    kb_ref[...] = k.astype(BF16)
    v = proj(C_V, C_QI)
    v_ref[...] = v
    vb_ref[...] = v.astype(BF16)
    for c in range(IDX_HEADS * IDX_DIM // LANES):
        lo = C_QI + c * LANES
        qi_ref[:, c * LANES:(c + 1) * LANES] = (rope(proj(lo, lo + LANES)) * scale).astype(BF16)
    z_ref[...] = proj(C_Z, C_XBC)
    xbc_ref[...] = proj(C_XBC, C_XP)
    xp_ref[...] = proj(C_XP, C_SM)
    sm = proj(C_SM, C_END)
    sm_ref[...] = sm
    ki = rope(sm)[:, :IDX_DIM]
    ki_ref[...] = ki
    kib_ref[...] = ki.astype(BF16)


def _inproj(x, ln, sc, sh, w, cos, sin, *, tl, rows_per_mod):
    n, d = x.shape
    nt = n // tl
    n_cos = cos.shape[0] // tl
    mod_rows = sc.shape[1]
    mod_spec = pl.BlockSpec((None, mod_rows, d), lambda i: (i // rows_per_mod, 0, 0))
    row = lambda width: pl.BlockSpec((tl, width), lambda i: (i, 0))
    outs = [(ATTN_DIM, BF16), (KV_DIM, F32), (KV_DIM, F32), (IDX_HEADS * IDX_DIM, BF16),
            (IDX_DIM, F32), (LANES, F32), (SSD_INNER, F32), (CONV_DIM, F32), (POOL_DIM, F32),
            (KV_DIM, BF16), (KV_DIM, BF16), (IDX_DIM, BF16)]
    return pl.pallas_call(
        _inproj_kernel,
        grid=(nt,),
        in_specs=[row(d),
                  pl.BlockSpec((1, d), lambda i: (0, 0)),
                  mod_spec, mod_spec,
                  pl.BlockSpec((d, C_END), lambda i: (0, 0)),
                  pl.BlockSpec((tl, LANES), lambda i: (i % n_cos, 0)),
                  pl.BlockSpec((tl, LANES), lambda i: (i % n_cos, 0))],
        out_specs=[row(wd) for wd, _ in outs],
        out_shape=[jax.ShapeDtypeStruct((n, wd), dt) for wd, dt in outs],
        compiler_params=_cparams(("arbitrary",)),
        name="inproj",
    )(x, ln, sc, sh, w, cos, sin)


def _sortable_key(score):
    score = jnp.where(score == 0.0, 0.0, score)
    bits = pltpu.bitcast(score, I32)
    return bits ^ ((bits >> 31) & 0x7FFFFFFF)


def _kth_largest_key(key_ref, kth):
    rows = key_ref.shape[0]

    def body(i, t_u):
        bit = jnp.left_shift(jnp.int32(1), 31 - i)
        cand_u = t_u | bit
        cand_s = cand_u ^ INT_MIN
        cnt = jnp.sum((key_ref[...] >= cand_s).astype(I32), axis=1, keepdims=True)
        return jnp.where(cnt >= kth, cand_u, t_u)

    t_u = lax.fori_loop(0, 32, body, jnp.zeros((rows, 1), I32))
    return t_u ^ INT_MIN


def _selection_bias(key_ref, bias_ref, tri_ref, allowed, kth, chunk):
    t = _kth_largest_key(key_ref, kth)
    n_gt = jnp.sum((key_ref[...] > t).astype(I32), axis=1, keepdims=True)
    quota = (kth - n_gt).astype(F32)
    run = jnp.zeros_like(quota)
    for c in range(key_ref.shape[1] // chunk):
        sl = slice(c * chunk, (c + 1) * chunk)
        kc = key_ref[:, sl]
        eq = jnp.where(kc == t, 1.0, 0.0)
        prefix = _dot(eq.astype(BF16), tri_ref[...]) + run
        run = prefix[:, chunk - 1:chunk]
        take = jnp.where(kc > t, 0.0, jnp.where(kc == t, jnp.where(prefix <= quota, 0.0, NEG_INF), NEG_INF))
        bias_ref[:, sl] = jnp.where(allowed[:, sl], take, NEG_INF)


def _upper_tri(n):
    r = jnp.arange(n)
    return (r[:, None] <= r[None, :]).astype(BF16)


def _attn_prompt_kernel(q_ref, qi_ref, sm_ref, k_ref, v_ref, ki_ref, tri_ref, o_ref,
                        key_s, bias_s, *, tq, seq, row0, kth):
    pos = row0 + lax.broadcasted_iota(I32, (tq, 1), 0)
    col = lax.broadcasted_iota(I32, (tq, seq), 1)
    causal = col <= pos
    if row0 + tq <= kth:
        bias_s[...] = jnp.where(causal, 0.0, NEG_INF)
    else:
        qi = qi_ref[...]
        ki = ki_ref[...]
        sm = sm_ref[...]
        score = jnp.zeros((tq, seq), F32)
        for h in range(IDX_HEADS):
            s = _dot_nt(qi[:, h * IDX_DIM:(h + 1) * IDX_DIM], ki)
            score = score + jnp.maximum(s, 0.0) * (sm[:, SM_WI + h:SM_WI + h + 1] * (IDX_HEADS ** -0.5))
        key_s[...] = _sortable_key(jnp.where(causal, score, NEG_INF))
        _selection_bias(key_s, bias_s, tri_ref, causal, kth, 256)

    q = q_ref[...]
    k = k_ref[...]
    v = v_ref[...]
    rep = ATTN_HEADS // N_KV_HEADS
    for h in range(ATTN_HEADS):
        g = h // rep
        logits = _dot_nt(q[:, h * HEAD_DIM:(h + 1) * HEAD_DIM], k[:, g * HEAD_DIM:(g + 1) * HEAD_DIM]) + bias_s[...]
        m = jnp.max(logits, axis=1, keepdims=True)
        e = jnp.exp(logits - m)
        den = jnp.sum(e, axis=1, keepdims=True)
        o = _dot(e.astype(BF16), v[:, g * HEAD_DIM:(g + 1) * HEAD_DIM]) / den
        o_ref[:, h * HEAD_DIM:(h + 1) * HEAD_DIM] = o.astype(BF16)


def _attn_prompt(q, qi, sm, kb, vb, kib, *, batch, seq, tq):
    nq = seq // tq
    kth = min(TOPK_MAX, seq // 4)
    tri = _upper_tri(256)
    seq3 = lambda t: t.reshape(batch, seq, t.shape[-1])
    kb3, vb3, kib3 = seq3(kb), seq3(vb), seq3(kib)
    outs = []
    for j in range(nq):
        keys = (j + 1) * tq
        qrow = lambda width, j=j: pl.BlockSpec((tq, width), lambda b: (b * nq + j, 0))
        head = lambda width, keys=keys: pl.BlockSpec((None, keys, width), lambda b: (b, 0, 0))
        outs.append(pl.pallas_call(
            functools.partial(_attn_prompt_kernel, tq=tq, seq=keys, row0=j * tq, kth=kth),
            grid=(batch,),
            in_specs=[qrow(ATTN_DIM), qrow(IDX_HEADS * IDX_DIM), qrow(LANES),
                      head(KV_DIM), head(KV_DIM), head(IDX_DIM),
                      pl.BlockSpec((256, 256), lambda b: (0, 0))],
            out_specs=pl.BlockSpec((None, tq, ATTN_DIM), lambda b: (b, 0, 0)),
            out_shape=jax.ShapeDtypeStruct((batch, tq, ATTN_DIM), BF16),
            scratch_shapes=[pltpu.VMEM((tq, keys), I32), pltpu.VMEM((tq, keys), F32)],
            compiler_params=_cparams(("arbitrary",)),
            name=f"attn_prompt_{j}",
        )(q, qi, sm, kb3, vb3, kib3, tri))
    return jnp.concatenate(outs, axis=1).reshape(batch * seq, ATTN_DIM)


T_PAD = 8
H_PAD = 8


def _paged_fetch(pt_ref, srcs, bufs, sems, n_pages):
    def copies(page, slot, p):
        off = pl.multiple_of(p * PAGE_SIZE, PAGE_SIZE)
        return [pltpu.make_async_copy(src.at[page], buf.at[slot, :, pl.ds(off, PAGE_SIZE)], sems.at[slot, i])
                for i, (src, buf) in enumerate(zip(srcs, bufs))]

    def start(seq, slot):
        def body(p, carry):
            for cp in copies(pt_ref[seq, p], slot, p):
                cp.start()
            return carry
        lax.fori_loop(0, n_pages, body, 0)

    def wait(seq, slot):
        def body(p, carry):
            for cp in copies(pt_ref[seq, p], slot, p):
                cp.wait()
            return carry
        lax.fori_loop(0, n_pages, body, 0)

    return start, wait


def _prefetch_next_wait_current(start, wait):
    b = pl.program_id(0)
    slot = b % 2

    @pl.when(b == 0)
    def _():
        start(0, 0)

    @pl.when(b + 1 < pl.num_programs(0))
    def _():
        start(b + 1, 1 - slot)

    wait(b, slot)
    return slot


def _idx_keys_kernel(pt_ref, qi_ref, wi_ref, kin_ref, ckidx_hbm, key_ref, kidx_buf, sems, *, n_pages):
    past = n_pages * PAGE_SIZE
    total = past + LANES
    start, wait = _paged_fetch(pt_ref, [ckidx_hbm], [kidx_buf], sems, n_pages)
    slot = _prefetch_next_wait_current(start, wait)
    qi = qi_ref[...]
    s = jnp.concatenate([_dot(qi, kidx_buf[slot].astype(BF16)), _dot(qi, kin_ref[...])], axis=1)
    weighted = jnp.maximum(s, 0.0) * (wi_ref[...] * (IDX_HEADS ** -0.5))
    score = jnp.sum(weighted.reshape(T_PAD, H_PAD, total), axis=1)
    qpos = past + lax.broadcasted_iota(I32, (T_PAD, 1), 0)
    col = lax.broadcasted_iota(I32, (T_PAD, total), 1)
    key_ref[...] = _sortable_key(jnp.where(col <= qpos, score, NEG_INF))


def _select_sample_kernel(key_ref, tri_ref, bias_ref, *, past, kth):
    rows, total = key_ref.shape
    qpos = past + lax.broadcasted_iota(I32, (rows, 1), 0) % T_PAD
    col = lax.broadcasted_iota(I32, (rows, total), 1)
    _selection_bias(key_ref, bias_ref, tri_ref, col <= qpos, kth, LANES)


def _attn_sample_kernel(pt_ref, q_ref, bias_ref, kn_ref, vn_ref, ck_hbm, cv_hbm, o_ref,
                        k_buf, v_buf, sems, *, n_pages):
    past = n_pages * PAGE_SIZE
    total = past + LANES
    start, wait = _paged_fetch(pt_ref, [ck_hbm, cv_hbm], [k_buf, v_buf], sems, n_pages)
    slot = _prefetch_next_wait_current(start, wait)
    q = q_ref[...]
    bias = jnp.broadcast_to(bias_ref[...][:, None, :], (T_PAD, ATTN_HEADS, total)).reshape(T_PAD * ATTN_HEADS, total)
    logits = jnp.concatenate([_dot(q, k_buf[slot].astype(BF16)), _dot(q, kn_ref[...])], axis=1) + bias
    m = jnp.max(logits, axis=1, keepdims=True)
    e = jnp.exp(logits - m)
    den = jnp.sum(e, axis=1, keepdims=True)
    eb = e.astype(BF16)
    o = _dot_nt(eb[:, :past], v_buf[slot].astype(BF16)) + _dot_nt(eb[:, past:], vn_ref[...])
    o_ref[...] = o / den


def _attn_sample(pages, qi2, wi2, q2, kn_t, vn_t, kin_t, ckidx_t, ck_t, cv_t, *, steps):
    bd, n_pages = pages.shape
    past = n_pages * PAGE_SIZE
    total = past + LANES
    kth = min(TOPK_MAX, (past + steps) // 4)
    per_b = lambda r, c: pl.BlockSpec((None, r, c), lambda b, pt: (b, 0, 0))
    hbm = pl.BlockSpec(memory_space=pl.ANY)

    keys = pl.pallas_call(
        functools.partial(_idx_keys_kernel, n_pages=n_pages),
        grid_spec=pltpu.PrefetchScalarGridSpec(
            num_scalar_prefetch=1, grid=(bd,),
            in_specs=[per_b(T_PAD * H_PAD, IDX_DIM), per_b(T_PAD * H_PAD, 1), per_b(IDX_DIM, LANES), hbm],
            out_specs=per_b(T_PAD, total),
            scratch_shapes=[pltpu.VMEM((2, IDX_DIM, past), F32), pltpu.SemaphoreType.DMA((2, 1))]),
        out_shape=jax.ShapeDtypeStruct((bd, T_PAD, total), I32),
        compiler_params=_cparams(("arbitrary",)),
        name="idx_keys",
    )(pages, qi2, wi2, kin_t, ckidx_t)

    rows = 64
    bias = pl.pallas_call(
        functools.partial(_select_sample_kernel, past=past, kth=kth),
        grid=(bd * T_PAD // rows,),
        in_specs=[pl.BlockSpec((rows, total), lambda i: (i, 0)),
                  pl.BlockSpec((LANES, LANES), lambda i: (0, 0))],
        out_specs=pl.BlockSpec((rows, total), lambda i: (i, 0)),
        out_shape=jax.ShapeDtypeStruct((bd * T_PAD, total), F32),
        compiler_params=_cparams(("arbitrary",)),
        name="select_sample",
    )(keys.reshape(bd * T_PAD, total), _upper_tri(LANES))

    return pl.pallas_call(
        functools.partial(_attn_sample_kernel, n_pages=n_pages),
        grid_spec=pltpu.PrefetchScalarGridSpec(
            num_scalar_prefetch=1, grid=(bd,),
            in_specs=[per_b(T_PAD * ATTN_HEADS, KV_DIM), per_b(T_PAD, total),
                      per_b(KV_DIM, LANES), per_b(KV_DIM, LANES), hbm, hbm],
            out_specs=per_b(T_PAD * ATTN_HEADS, KV_DIM),
            scratch_shapes=[pltpu.VMEM((2, KV_DIM, past), F32), pltpu.VMEM((2, KV_DIM, past), F32),
                            pltpu.SemaphoreType.DMA((2, 2))]),
        out_shape=jax.ShapeDtypeStruct((bd, T_PAD * ATTN_HEADS, KV_DIM), F32),
        compiler_params=_cparams(("arbitrary",)),
        name="attn_sample",
    )(pages, q2, bias.reshape(bd, T_PAD, total), kn_t, vn_t, ck_t, cv_t)


def _expand_heads(v, first_lane):
    q = v.shape[0]
    lane = lax.broadcasted_iota(I32, (1, SSD_INNER), 1)
    out = jnp.broadcast_to(v[:, first_lane + SSD_HEADS - 1:first_lane + SSD_HEADS], (q, SSD_INNER))
    for h in range(SSD_HEADS - 2, -1, -1):
        out = jnp.where(lane < (h + 1) * SSD_HEAD_DIM, v[:, first_lane + h:first_lane + h + 1], out)
    return out


def _ssd_kernel(xbc_ref, z_ref, sm_ref, cbuf_ref, st0_ref, cw_ref, cb_ref, dtb_ref, alog_ref,
                dsk_ref, nw_ref, y_ref, stout_ref, xe_s, st_s, *, q, valid_len):
    c = pl.program_id(1)

    @pl.when(c == 0)
    def _():
        xe_s[0:8] = cbuf_ref[...]
        st_s[...] = st0_ref[...]

    xe_s[8:8 + q] = xbc_ref[...]
    acc = cb_ref[...]
    for j in range(CONV_W):
        acc = acc + xe_s[pl.ds(8 - (CONV_W - 1) + j, q), :] * cw_ref[j:j + 1, :]
    xe_s[0:8] = xe_s[q:q + 8]
    u = _silu(acc)
    xs = u[:, :SSD_INNER]
    bm = u[:, SSD_INNER:SSD_INNER + SSD_STATE].astype(BF16)
    cm = u[:, SSD_INNER + SSD_STATE:].astype(BF16)

    lane = lax.broadcasted_iota(I32, (1, LANES), 1)
    head_lanes = (lane >= SM_DT) & (lane < SM_DT + SSD_HEADS)
    pre = sm_ref[...] + dtb_ref[...]
    dt = jnp.maximum(pre, 0.0) + jnp.log1p(jnp.exp(-jnp.abs(pre)))
    dt = jnp.where(head_lanes, dt, 0.0)
    if valid_len < q:
        row_id = lax.broadcasted_iota(I32, (q, 1), 0)
        dt = jnp.where(row_id < valid_len, dt, 0.0)
    a = dt * jnp.where(head_lanes, -jnp.exp(alog_ref[...]), 0.0)

    r = lax.broadcasted_iota(I32, (q, q), 0)
    cidx = lax.broadcasted_iota(I32, (q, q), 1)
    lower = r >= cidx
    a_cs = _dot(jnp.where(lower, 1.0, 0.0), a, precision=lax.Precision.HIGHEST)
    a_cs_t = a_cs.T
    a_last = a_cs[q - 1:q, :]

    xdt = xs * _expand_heads(dt, SM_DT)
    cb = _dot_nt(cm, bm)
    lane_in = lax.broadcasted_iota(I32, (1, SSD_INNER), 1)
    y = jnp.zeros((q, SSD_INNER), F32)
    for h in range(SSD_HEADS):
        seg = a_cs[:, SM_DT + h:SM_DT + h + 1] - a_cs_t[SM_DT + h:SM_DT + h + 1, :]
        lmat = jnp.exp(jnp.where(lower, seg, NEG_INF))
        in_head = (lane_in >= h * SSD_HEAD_DIM) & (lane_in < (h + 1) * SSD_HEAD_DIM)
        y = y + _dot((cb * lmat).astype(BF16), jnp.where(in_head, xdt, 0.0).astype(BF16))

    st = st_s[...]
    y = y + _dot_nt(cm, st.astype(BF16)) * _expand_heads(jnp.exp(a_cs), SM_DT)
    decayed = xdt * _expand_heads(jnp.exp(a_last - a_cs), SM_DT)
    contrib = _dot(decayed.T.astype(BF16), bm)
    carry = jnp.exp(a_last)
    row_h = lax.broadcasted_iota(I32, (SSD_INNER, 1), 0) // SSD_HEAD_DIM
    fac = jnp.broadcast_to(carry[:, SM_DT + SSD_HEADS - 1:SM_DT + SSD_HEADS], (SSD_INNER, SSD_STATE))
    for h in range(SSD_HEADS - 2, -1, -1):
        fac = jnp.where(row_h == h, jnp.broadcast_to(carry[:, SM_DT + h:SM_DT + h + 1], (SSD_INNER, SSD_STATE)), fac)
    st_new = st * fac + contrib
    st_s[...] = st_new

    y = y + dsk_ref[...] * xs
    gated = y * _silu(z_ref[...])
    out = gated * lax.rsqrt(jnp.mean(gated * gated, axis=-1, keepdims=True) + EPS) * nw_ref[...]
    y_ref[...] = out.astype(BF16)

    @pl.when(c == pl.num_programs(1) - 1)
    def _():
        stout_ref[...] = st_new


def _ssd(xbc, z, sm, cbuf, st0, cw, cb, dtb, alog, dsk, nw, *, batch, seq, q, valid_len):
    n = xbc.shape[0]
    nc = seq // q
    row = lambda width: pl.BlockSpec((q, width), lambda b, c: (b * nc + c, 0))
    const = lambda r, w: pl.BlockSpec((r, w), lambda b, c: (0, 0))
    return pl.pallas_call(
        functools.partial(_ssd_kernel, q=q, valid_len=valid_len),
        grid=(batch, nc),
        in_specs=[row(CONV_DIM), row(SSD_INNER), row(LANES),
                  pl.BlockSpec((None, 8, CONV_DIM), lambda b, c: (b, 0, 0)),
                  pl.BlockSpec((None, SSD_INNER, SSD_STATE), lambda b, c: (b, 0, 0)),
                  const(CONV_W, CONV_DIM), const(1, CONV_DIM), const(1, LANES), const(1, LANES),
                  const(1, SSD_INNER), const(1, SSD_INNER)],
        out_specs=[row(SSD_INNER),
                   pl.BlockSpec((None, SSD_INNER, SSD_STATE), lambda b, c: (b, 0, 0))],
        out_shape=[jax.ShapeDtypeStruct((n, SSD_INNER), BF16),
                   jax.ShapeDtypeStruct((batch, SSD_INNER, SSD_STATE), F32)],
        scratch_shapes=[pltpu.VMEM((q + 8, CONV_DIM), F32), pltpu.VMEM((SSD_INNER, SSD_STATE), F32)],
        compiler_params=_cparams(("arbitrary", "arbitrary")),
        name="ssd",
    )(xbc, z, sm, cbuf, st0, cw, cb, dtb, alog, dsk, nw)


def _window_sums(e):
    s2 = e[0] + e[1]
    s4 = s2 + e[2] + e[3]
    s8 = s4 + e[4] + e[5] + e[6] + e[7]
    s16 = s8
    for k in range(8, 16):
        s16 = s16 + e[k]
    lane = lax.broadcasted_iota(I32, (1, POOL_DIM), 1)
    g = POOL_GROUP_DIM
    ssum = jnp.where(lane < g, s2, jnp.where(lane < 2 * g, s4, jnp.where(lane < 3 * g, s8, s16)))
    width = jnp.where(lane < g, 2.0, jnp.where(lane < 2 * g, 4.0, jnp.where(lane < 3 * g, 8.0, 16.0)))
    return ssum, width


def _pool_prompt_kernel(xp_ref, w_ref, scale_ref, o_ref, ext_s, *, tl):
    i = pl.program_id(1)
    halo = POOL_BUF + 1

    @pl.when(i == 0)
    def _():
        ext_s[0:halo] = jnp.zeros((halo, POOL_DIM), F32)

    ext_s[halo:halo + tl] = xp_ref[...]
    e = [ext_s[pl.ds(halo - k, tl), :] for k in range(halo)]
    ssum, width = _window_sums(e)
    t = (i * tl + lax.broadcasted_iota(I32, (tl, 1), 0)).astype(F32)
    count = jnp.minimum(t + 1.0, width)
    pooled = ssum / count - e[0]
    o_ref[...] = (_dot(pooled.astype(BF16), w_ref[...]) * scale_ref[...]).astype(BF16)
    ext_s[0:halo] = ext_s[tl:tl + halo]


def _pool_prompt(xp, w_bd, scale, *, batch, seq, tl):
    n = xp.shape[0]
    nt = seq // tl
    return pl.pallas_call(
        functools.partial(_pool_prompt_kernel, tl=tl),
        grid=(batch, nt),
        in_specs=[pl.BlockSpec((tl, POOL_DIM), lambda b, i: (b * nt + i, 0)),
                  pl.BlockSpec((POOL_DIM, POOL_DIM), lambda b, i: (0, 0)),
                  pl.BlockSpec((1, POOL_DIM), lambda b, i: (0, 0))],
        out_specs=pl.BlockSpec((tl, POOL_DIM), lambda b, i: (b * nt + i, 0)),
        out_shape=jax.ShapeDtypeStruct((n, POOL_DIM), BF16),
        scratch_shapes=[pltpu.VMEM((tl + POOL_BUF + 1, POOL_DIM), F32)],
        compiler_params=_cparams(("arbitrary", "arbitrary")),
        name="pool_prompt",
    )(xp, w_bd, scale)


def _pool_sample_kernel(ext_ref, w_ref, scale_ref, o_ref, *, steps):
    e = [ext_ref[POOL_BUF - k:POOL_BUF - k + steps] for k in range(POOL_BUF + 1)]
    ssum, width = _window_sums(e)
    pooled = ssum / width - e[0]
    rows = pooled.shape[0] * pooled.shape[1]
    mixed = _dot(pooled.reshape(rows, POOL_DIM).astype(BF16), w_ref[...])
    o_ref[...] = (mixed * scale_ref[...]).astype(BF16)


def _pool_sample(ext_t, w_bd, scale, *, steps):
    rows = steps * ext_t.shape[1]
    return pl.pallas_call(
        functools.partial(_pool_sample_kernel, steps=steps),
        out_shape=jax.ShapeDtypeStruct((rows, POOL_DIM), BF16),
        compiler_params=pltpu.CompilerParams(vmem_limit_bytes=VMEM_LIMIT),
        name="pool_sample",
    )(ext_t, w_bd, scale)


def _mix_ffn_kernel(x_ref, a_ref, s_ref, p_ref, wo_ref, g1_ref, ln_ref, sc_ref, sh_ref, g2_ref,
                    wg_ref, wu_ref, wd_ref, lnf_ref, o_ref, x1_s, h2_s, acc_s, *, final_norm):
    f = pl.program_id(1)

    @pl.when(f == 0)
    def _():
        mix = (_dot(a_ref[...], wo_ref[0:ATTN_DIM, :])
               + _dot(s_ref[...], wo_ref[ATTN_DIM:ATTN_DIM + SSD_INNER, :])
               + _dot(p_ref[...], wo_ref[ATTN_DIM + SSD_INNER:, :]))
        x1 = x_ref[...] + g1_ref[...] * mix
        x1_s[...] = x1
        h2_s[...] = _rms_mod(x1, ln_ref[...], sc_ref[...], sh_ref[...]).astype(BF16)
        acc_s[...] = jnp.zeros_like(acc_s)

    h2 = h2_s[...]
    act = (_silu(_dot(h2, wg_ref[...])) * _dot(h2, wu_ref[...])).astype(BF16)
    acc_s[...] += _dot(act, wd_ref[...])

    @pl.when(f == pl.num_programs(1) - 1)
    def _():
        out = x1_s[...] + g2_ref[...] * acc_s[...]
        if final_norm:
            out = out * lax.rsqrt(jnp.mean(out * out, axis=-1, keepdims=True) + EPS) * lnf_ref[...]
        o_ref[...] = out


def _mix_ffn(x, a, s, p, wo, g1, ln, sc, sh, g2, wg, wu, wd, lnf, *, tl, rows_per_mod, nf, final_norm):
    n, d = x.shape
    nt = n // tl
    tf = D_FF // nf
    mod_rows = g1.shape[1]
    mod_spec = pl.BlockSpec((None, mod_rows, d), lambda i, f: (i // rows_per_mod, 0, 0))
    row = lambda width: pl.BlockSpec((tl, width), lambda i, f: (i, 0))
    vec = pl.BlockSpec((1, d), lambda i, f: (0, 0))
    return pl.pallas_call(
        functools.partial(_mix_ffn_kernel, final_norm=final_norm),
        grid=(nt, nf),
        in_specs=[row(d), row(ATTN_DIM), row(SSD_INNER), row(POOL_DIM),
                  pl.BlockSpec((d, d), lambda i, f: (0, 0)),
                  mod_spec, vec, mod_spec, mod_spec, mod_spec,
                  pl.BlockSpec((d, tf), lambda i, f: (0, f)),
                  pl.BlockSpec((d, tf), lambda i, f: (0, f)),
                  pl.BlockSpec((tf, d), lambda i, f: (f, 0)),
                  vec],
        out_specs=row(d),
        out_shape=jax.ShapeDtypeStruct((n, d), F32),
        scratch_shapes=[pltpu.VMEM((tl, d), F32), pltpu.VMEM((tl, d), BF16), pltpu.VMEM((tl, d), F32)],
        compiler_params=_cparams(("arbitrary", "arbitrary")),
        name="mix_ffn",
    )(x, a, s, p, wo, g1, ln, sc, sh, g2, wg, wu, wd, lnf)


def _rope_tables(pos):
    half = HEAD_DIM // 2
    inv = ROPE_THETA ** (-jnp.arange(half, dtype=F32) / half)
    ang = pos.astype(F32)[:, None] * inv[None, :]
    cos, sin = jnp.cos(ang), jnp.sin(ang)
    cos_t = jnp.concatenate([cos, cos, cos, cos], axis=1)
    sin_t = jnp.concatenate([-sin, sin, -sin, sin], axis=1)
    return cos_t, sin_t


def _permute_w_in(w_in):
    sizes = (ATTN_DIM, KV_DIM, KV_DIM, IDX_HEADS * IDX_DIM, IDX_DIM, IDX_HEADS,
             SSD_INNER, CONV_DIM, SSD_HEADS, POOL_DIM)
    offs = [0]
    for s in sizes:
        offs.append(offs[-1] + s)
    seg = lambda i: w_in[:, offs[i]:offs[i + 1]]
    q, k, v, qi, ki, wi, z, xbc, dtr, xp = [seg(i) for i in range(len(sizes))]
    pad = jnp.zeros((w_in.shape[0], C_END - C_SM - IDX_DIM - IDX_HEADS - SSD_HEADS), w_in.dtype)
    return jnp.concatenate([q, k, v, qi, z, xbc, xp, ki, wi, dtr, pad], axis=1).astype(BF16)


def _block_diag(pool_w):
    g, c, _ = pool_w.shape
    out = jnp.zeros((g * c, g * c), pool_w.dtype)
    for i in range(g):
        out = out.at[i * c:(i + 1) * c, i * c:(i + 1) * c].set(pool_w[i])
    return out.astype(BF16)


def _head_lane_vec(v):
    return jnp.zeros((1, LANES), F32).at[0, SM_DT:SM_DT + SSD_HEADS].set(v)


def kernel(x_prompt, x_sample, cache_k, cache_v, cache_kidx, state_ssm, state_conv, state_pool,
           page_table, c_prompt, c_sample, ln_mix, ln_ffn, w_ada, b_ada, w_in, conv_w, conv_b,
           dt_bias, a_log, d_skip, ssd_norm, pool_w, pool_scale, w_out, w_gate, w_up, w_down,
           ln_final):
    bp, lp, d = x_prompt.shape
    bs, ls, _ = x_sample.shape
    depth = w_in.shape[0]
    n_pages = page_table.shape[1]
    past = n_pages * PAGE_SIZE
    n_p, n_s = bp * lp, bs * ls
    tl_p = 512
    ssd_q = 128

    cos_p, sin_p = _rope_tables(jnp.arange(lp))
    cos_s, sin_s = _rope_tables(past + jnp.arange(ls))
    cos_s, sin_s = jnp.tile(cos_s, (bs, 1)), jnp.tile(sin_s, (bs, 1))

    n_pool = cache_k.shape[1]
    ck_t = jnp.transpose(cache_k, (0, 1, 3, 4, 2)).reshape(depth * n_pool, KV_DIM, PAGE_SIZE)
    cv_t = jnp.transpose(cache_v, (0, 1, 3, 4, 2)).reshape(depth * n_pool, KV_DIM, PAGE_SIZE)
    ckidx_t = jnp.transpose(cache_kidx, (0, 1, 3, 2)).reshape(depth * n_pool, IDX_DIM, PAGE_SIZE)

    mod = _ada(jnp.concatenate([c_prompt, c_sample], axis=0), w_ada, b_ada)
    lnf = ln_final.reshape(1, d)

    xp = x_prompt.reshape(n_p, d)
    xs = x_sample.reshape(n_s, d)
    outs_p = [[] for _ in range(6)]
    outs_s = [[] for _ in range(6)]
    for l in range(depth):
        last = l == depth - 1
        mods_p = [m.reshape(bp, 1, d) for m in jnp.split(mod[l, :bp], 6, axis=-1)]
        mods_s = [jnp.repeat(m, ls, axis=0).reshape(1, n_s, d) for m in jnp.split(mod[l, bp:], 6, axis=-1)]
        w_in_p = _permute_w_in(w_in[l])
        wo = w_out[l].astype(BF16)
        wg, wu, wdn = w_gate[l].astype(BF16), w_up[l].astype(BF16), w_down[l].astype(BF16)
        w_bd = _block_diag(pool_w[l])
        p_scale = pool_scale[l].reshape(1, POOL_DIM)
        ln1, ln2 = ln_mix[l].reshape(1, d), ln_ffn[l].reshape(1, d)
        ssd_w = (conv_w[l], conv_b[l].reshape(1, CONV_DIM), _head_lane_vec(dt_bias[l]), _head_lane_vec(a_log[l]),
                 jnp.repeat(d_skip[l], SSD_HEAD_DIM).reshape(1, SSD_INNER), ssd_norm[l].reshape(1, SSD_INNER))

        sh1, sc1, g1, sh2, sc2, g2 = mods_p
        q, k, v, qi, ki, sm, z, xbc, xpool, kb, vb, kib = _inproj(
            xp, ln1, sc1, sh1, w_in_p, cos_p, sin_p, tl=tl_p, rows_per_mod=lp // tl_p)
        a_out = _attn_prompt(q, qi, sm, kb, vb, kib, batch=bp, seq=lp, tq=256)
        s_out, ssm = _ssd(xbc, z, sm, jnp.zeros((bp, 8, CONV_DIM), F32),
                          jnp.zeros((bp, SSD_INNER, SSD_STATE), F32), *ssd_w,
                          batch=bp, seq=lp, q=ssd_q, valid_len=ssd_q)
        p_out = _pool_prompt(xpool, w_bd, p_scale, batch=bp, seq=lp, tl=tl_p)
        xp = _mix_ffn(xp, a_out, s_out, p_out, wo, g1, ln2, sc2, sh2, g2, wg, wu, wdn, lnf,
                      tl=tl_p, rows_per_mod=lp // tl_p, nf=2, final_norm=last)
        outs_p[0].append(k.reshape(bp, lp, N_KV_HEADS, HEAD_DIM))
        outs_p[1].append(v.reshape(bp, lp, N_KV_HEADS, HEAD_DIM))
        outs_p[2].append(ki.reshape(bp, lp, IDX_DIM))
        outs_p[3].append(ssm.reshape(bp, SSD_HEADS, SSD_HEAD_DIM, SSD_STATE))
        outs_p[4].append(xbc.reshape(bp, lp, CONV_DIM)[:, lp - (CONV_W - 1):])
        outs_p[5].append(xpool.reshape(bp, lp, POOL_DIM)[:, lp - POOL_BUF:])

        sh1, sc1, g1, sh2, sc2, g2 = mods_s
        q, k, v, qi, ki, sm, z, xbc, xpool, kb, vb, kib = _inproj(
            xs, ln1, sc1, sh1, w_in_p, cos_s, sin_s, tl=n_s, rows_per_mod=1)
        tpad = ((0, 0), (0, T_PAD - ls))
        qi2 = jnp.pad(qi.reshape(bs, ls, IDX_HEADS, IDX_DIM), tpad + ((0, H_PAD - IDX_HEADS), (0, 0)))
        qi2 = qi2.reshape(bs, T_PAD * H_PAD, IDX_DIM)
        wi2 = jnp.pad(sm[:, SM_WI:SM_WI + IDX_HEADS].reshape(bs, ls, IDX_HEADS), tpad + ((0, H_PAD - IDX_HEADS),))
        wi2 = wi2.reshape(bs, T_PAD * H_PAD, 1)
        qh = jnp.pad(q.reshape(bs, ls, N_KV_HEADS, ATTN_HEADS // N_KV_HEADS, HEAD_DIM), tpad + ((0, 0),) * 3)
        eye = jnp.eye(N_KV_HEADS, dtype=q.dtype)
        q2 = jnp.einsum('btgrd,gj->btgrjd', qh, eye).reshape(bs, T_PAD * ATTN_HEADS, KV_DIM)
        new_t = lambda t: jnp.pad(jnp.transpose(t.reshape(bs, ls, -1), (0, 2, 1)), ((0, 0), (0, 0), (0, LANES - ls)))
        o2 = _attn_sample(page_table + l * n_pool, qi2, wi2, q2, new_t(kb), new_t(vb), new_t(kib),
                          ckidx_t, ck_t, cv_t, steps=ls)
        o2 = o2.reshape(bs, T_PAD, N_KV_HEADS, ATTN_HEADS // N_KV_HEADS, N_KV_HEADS, HEAD_DIM)[:, :ls]
        a_out = jnp.einsum('btgrjd,gj->btgrd', o2, jnp.eye(N_KV_HEADS, dtype=o2.dtype))
        a_out = a_out.reshape(n_s, ATTN_DIM).astype(BF16)
        seq_pad = lambda t: jnp.pad(t.reshape(bs, ls, -1), ((0, 0), (0, ssd_q - ls), (0, 0))).reshape(bs * ssd_q, -1)
        s_pad, ssm = _ssd(seq_pad(xbc), seq_pad(z), seq_pad(sm),
                          jnp.pad(state_conv[l], ((0, 0), (8 - (CONV_W - 1), 0), (0, 0))),
                          state_ssm[l].reshape(bs, SSD_INNER, SSD_STATE), *ssd_w,
                          batch=bs, seq=ssd_q, q=ssd_q, valid_len=ls)
        s_out = s_pad.reshape(bs, ssd_q, SSD_INNER)[:, :ls].reshape(n_s, SSD_INNER)
        ext = jnp.concatenate([state_pool[l], xpool.reshape(bs, ls, POOL_DIM)], axis=1)
        p_t = _pool_sample(jnp.transpose(ext, (1, 0, 2)), w_bd, p_scale, steps=ls)
        p_out = jnp.transpose(p_t.reshape(ls, bs, POOL_DIM), (1, 0, 2)).reshape(n_s, POOL_DIM)
        xs = _mix_ffn(xs, a_out, s_out, p_out, wo, g1, ln2, sc2, sh2, g2, wg, wu, wdn, lnf,
                      tl=n_s, rows_per_mod=1, nf=2, final_norm=last)
        outs_s[0].append(k.reshape(bs, ls, N_KV_HEADS, HEAD_DIM))
        outs_s[1].append(v.reshape(bs, ls, N_KV_HEADS, HEAD_DIM))
        outs_s[2].append(ki.reshape(bs, ls, IDX_DIM))
        outs_s[3].append(ssm.reshape(bs, SSD_HEADS, SSD_HEAD_DIM, SSD_STATE))
        outs_s[4].append(xbc.reshape(bs, ls, CONV_DIM)[:, ls - (CONV_W - 1):])
        outs_s[5].append(ext[:, ext.shape[1] - POOL_BUF:])

    return ((xp.reshape(bp, lp, d), xs.reshape(bs, ls, d))
            + tuple(jnp.stack(o) for o in outs_p) + tuple(jnp.stack(o) for o in outs_s))
```
